```python
import jax, jax.numpy as jnp
from jax import lax
import numpy as np

D_MODEL = 1024
BATCH = 2
SEQ = 8192
DEPTH = 2
DEC_BATCH = 32
DEC_SEQ = 4
PAST_LEN = 8192
PAGE_SIZE = 128

HEAD_DIM = 64
A_WIDTH = D_MODEL // 2
A_HEADS = A_WIDTH // HEAD_DIM
LORA_DECAY = 64
LORA_ICLR = 64
LORA_GATE = 128
A_SPLITS = (A_WIDTH, 2 * A_WIDTH, 3 * A_WIDTH, 3 * A_WIDTH + LORA_DECAY, 3 * A_WIDTH + LORA_DECAY + LORA_ICLR)
A_COLS = 3 * A_WIDTH + LORA_DECAY + LORA_ICLR + LORA_GATE
B_WIDTH = D_MODEL // 2
B_HEADS = B_WIDTH // HEAD_DIM
MOBA_BLOCK = 256
MOBA_TOPK = 3
Q_BLOCK = 128
ATTN_SCALE = HEAD_DIM ** -0.5
IN0_COLS = A_COLS + 3 * B_WIDTH
MIX_WIDTH = A_WIDTH + B_WIDTH
D_FF = 256 * (-(-8 * D_MODEL // (3 * 256)))
N_EXPERTS = 8
MOE_TOPK = 2
D_FF_EXPERT = 7 * D_MODEL // 2
MOE_BLOCK = 128
CONV_K = 31
RMS_EPS = 1e-6
GN_EPS = 64e-5
LN_EPS = 1e-5

kernel_name = 'rwkv7_moba_conformer_moe_decode_step'


def rms_norm(x, g):
    xf = x.astype(jnp.float32)
    y = xf * lax.rsqrt(jnp.mean(xf * xf, axis=-1, keepdims=True) + RMS_EPS)
    return (y * g.astype(jnp.float32)).astype(x.dtype)


def alibi_slopes():
    return 2.0 ** (-8.0 * jnp.arange(1, B_HEADS + 1, dtype=jnp.float32) / B_HEADS)


def swiglu(h, w_gate, w_up, w_down):
    return (jax.nn.silu(h @ w_gate) * (h @ w_up)) @ w_down


def rwkv7_time_mix(p_a, shift_prev, wkv_prev, mu_shift, w_lora_up, w0, a_lora_up, a0,
                   g_lora_up, k_k, k_a, r_k, lnx_g, lnx_b):
    B, T, _ = p_a.shape
    f32 = jnp.float32
    prev = jnp.concatenate([shift_prev[:, None, :].astype(p_a.dtype), p_a[:, :-1]], axis=1)
    xs = p_a + (prev - p_a) * mu_shift.astype(p_a.dtype)
    r, k, v, xw, xa, xg = jnp.split(xs, A_SPLITS, axis=-1)
    heads = lambda t: t.astype(f32).reshape(B, T, A_HEADS, HEAD_DIM)
    w_log = -jax.nn.softplus(-(w0 + jnp.tanh(xw) @ w_lora_up).astype(f32)) - 0.5
    decay = heads(jnp.exp(-jnp.exp(w_log)))
    a = heads(jax.nn.sigmoid((a0 + xa @ a_lora_up).astype(f32)))
    g = jax.nn.sigmoid(xg) @ g_lora_up
    kk = heads(k * k_k)
    kk = kk / jnp.maximum(jnp.sqrt(jnp.sum(kk * kk, axis=-1, keepdims=True)), 1e-12)
    k_h = heads(k) * (1.0 + (a - 1.0) * k_a.astype(f32).reshape(A_HEADS, HEAD_DIM))
    r_h = heads(r)
    v_h = heads(v)

    def step(S, inp):
        r_t, w_t, k_t, v_t, kk_t, a_t = inp
        s_kk = jnp.einsum('bhvk,bhk->bhv', S, kk_t)
        S = (S * w_t[:, :, None, :]
             - s_kk[..., None] * (kk_t * a_t)[:, :, None, :]
             + v_t[..., None] * k_t[:, :, None, :])
        return S, jnp.einsum('bhvk,bhk->bhv', S, r_t)

    seq_first = lambda t: jnp.moveaxis(t, 1, 0)
    wkv_new, y = lax.scan(step, wkv_prev.astype(f32),
                          tuple(seq_first(t) for t in (r_h, decay, k_h, v_h, kk, a)))
    y = jnp.moveaxis(y, 0, 1)
    mu = jnp.mean(y, axis=-1, keepdims=True)
    var = jnp.mean(jnp.square(y - mu), axis=-1, keepdims=True)
    y = ((y - mu) * lax.rsqrt(var + GN_EPS)).reshape(B, T, A_WIDTH) * lnx_g.astype(f32) + lnx_b.astype(f32)
    bonus = jnp.sum(r_h * k_h * r_k.astype(f32), axis=-1, keepdims=True) * v_h
    out = (y + bonus.reshape(B, T, A_WIDTH)) * g.astype(f32)
    return out.astype(p_a.dtype), wkv_new, p_a[:, -1]


def moba_blocks(k, v):
    B, L, H, D = k.shape
    nb = -(-L // MOBA_BLOCK)
    padw = ((0, 0), (0, nb * MOBA_BLOCK - L), (0, 0), (0, 0))
    kb = jnp.pad(k, padw).reshape(B, nb, MOBA_BLOCK, H, D).transpose(0, 3, 1, 2, 4)
    vb = jnp.pad(v, padw).reshape(B, nb, MOBA_BLOCK, H, D).transpose(0, 3, 1, 2, 4)
    km = jnp.mean(kb.astype(jnp.float32), axis=3)
    return kb, vb, km


def moba_query(q, q_pos, kb, vb, km, slopes):
    B, Q, H, D = q.shape
    nb = kb.shape[2]
    n_sel = min(MOBA_TOPK, nb)
    q_blk = q_pos // MOBA_BLOCK
    gate = jnp.einsum('bqhd,bhnd->bqhn', q.astype(jnp.float32), km)
    fully_past = jnp.arange(nb)[None, :] < q_blk[:, None]
    gate = jnp.where(fully_past[None, :, None, :], gate, -jnp.inf)
    _, top_blk = lax.top_k(gate, n_sel)
    own = jnp.broadcast_to(q_blk[None, :, None, None], (B, Q, H, 1)).astype(top_blk.dtype)
    sel = jnp.concatenate([top_blk, own], axis=-1)
    rank_ok = jnp.arange(n_sel)[None, :] < q_blk[:, None]
    sel_ok = jnp.concatenate([rank_ok, jnp.ones((Q, 1), bool)], axis=-1)[None, :, None, :]
    b_i = jnp.arange(B)[:, None, None, None]
    h_i = jnp.arange(H)[None, None, :, None]
    kg = kb[b_i, h_i, sel]
    vg = vb[b_i, h_i, sel]
    key_pos = sel[..., None] * MOBA_BLOCK + jnp.arange(MOBA_BLOCK, dtype=sel.dtype)
    dist = q_pos[None, :, None, None, None] - key_pos
    s = (jnp.einsum('bqhd,bqhnkd->bqhnk', q, kg).astype(jnp.float32) * ATTN_SCALE
         - slopes[None, None, :, None, None] * dist.astype(jnp.float32))
    s = jnp.where(sel_ok[..., None] & (dist >= 0), s, -jnp.inf)
    p = jax.nn.softmax(s.reshape(B, Q, H, -1), axis=-1).reshape(s.shape)
    return jnp.einsum('bqhnk,bqhnkd->bqhd', p.astype(vg.dtype), vg)


def layer0_mixer(h, shift_prev, wkv_prev, kv_past, q_pos, slopes, w_in0, rwkv_p, q_norm_g, k_norm_g, w_out0):
    B, T, _ = h.shape
    proj = h @ w_in0
    y_a, wkv_new, shift_new = rwkv7_time_mix(proj[..., :A_COLS], shift_prev, wkv_prev, *rwkv_p)
    qkv = proj[..., A_COLS:].reshape(B, T, 3, B_HEADS, HEAD_DIM)
    q = rms_norm(qkv[:, :, 0], q_norm_g)
    k = rms_norm(qkv[:, :, 1], k_norm_g)
    v = qkv[:, :, 2]
    if kv_past is None:
        kb, vb, km = moba_blocks(k, v)
        n_qb = T // Q_BLOCK
        q_blocks = jnp.moveaxis(q.reshape(B, n_qb, Q_BLOCK, B_HEADS, HEAD_DIM), 1, 0)
        pos_blocks = jnp.arange(T, dtype=jnp.int32).reshape(n_qb, Q_BLOCK)
        o = lax.map(lambda qp: moba_query(qp[0], qp[1], kb, vb, km, slopes), (q_blocks, pos_blocks))
        o = jnp.moveaxis(o, 0, 1).reshape(B, T, B_WIDTH)
    else:
        k_past, v_past = kv_past
        kb, vb, km = moba_blocks(jnp.concatenate([k_past.astype(k.dtype), k], axis=1),
                                 jnp.concatenate([v_past.astype(v.dtype), v], axis=1))
        o = moba_query(q, q_pos, kb, vb, km, slopes).reshape(B, T, B_WIDTH)
    out = jnp.concatenate([y_a, o.astype(y_a.dtype)], axis=-1) @ w_out0
    return out, k, v, wkv_new, shift_new


def conformer_conv(h, buf, w_in, dw_w, dw_b, ln_g, ln_b, w_out):
    u = h @ w_in
    u = u[..., :D_MODEL] * jax.nn.sigmoid(u[..., D_MODEL:])
    if buf is None:
        ctx = jnp.pad(u, ((0, 0), (CONV_K - 1, 0), (0, 0)))
    else:
        ctx = jnp.concatenate([buf.astype(u.dtype), u], axis=1)
    c = lax.conv_general_dilated(ctx, dw_w.astype(u.dtype)[:, None, :], window_strides=(1,), padding='VALID',
                                 dimension_numbers=('NWC', 'WIO', 'NWC'), feature_group_count=D_MODEL) + dw_b
    cf = c.astype(jnp.float32)
    mu = jnp.mean(cf, axis=-1, keepdims=True)
    var = jnp.mean(jnp.square(cf - mu), axis=-1, keepdims=True)
    cf = (cf - mu) * lax.rsqrt(var + LN_EPS) * ln_g.astype(jnp.float32) + ln_b.astype(jnp.float32)
    out = jax.nn.silu(cf).astype(h.dtype) @ w_out
    return out, ctx[:, -(CONV_K - 1):]


def moe_swiglu(h, w_router, w_gate, w_up, w_down):
    lead = h.shape[:-1]
    xt = h.reshape(-1, D_MODEL)
    n = xt.shape[0]
    probs = jax.nn.softmax((xt @ w_router).astype(jnp.float32), axis=-1)
    top_p, top_e = lax.top_k(probs, MOE_TOPK)
    top_p = top_p / jnp.sum(top_p, axis=-1, keepdims=True)
    flat_e = top_e.reshape(-1)
    flat_tok = jnp.repeat(jnp.arange(n, dtype=jnp.int32), MOE_TOPK)
    flat_p = top_p.reshape(-1)
    order = jnp.argsort(flat_e)
    e_sorted = flat_e[order]
    tok_sorted = flat_tok[order]
    p_sorted = flat_p[order]
    counts = jnp.bincount(flat_e, length=N_EXPERTS)
    padded = (counts + MOE_BLOCK - 1) // MOE_BLOCK * MOE_BLOCK
    start = jnp.cumsum(counts) - counts
    pstart = jnp.cumsum(padded) - padded
    slot = pstart[e_sorted] + jnp.arange(n * MOE_TOPK) - start[e_sorted]
    n_blocks = -(-(n * MOE_TOPK) // MOE_BLOCK) + N_EXPERTS
    buf_tok = jnp.zeros((n_blocks * MOE_BLOCK,), jnp.int32).at[slot].set(tok_sorted)
    block_e = jnp.minimum(jnp.searchsorted(jnp.cumsum(padded), jnp.arange(n_blocks) * MOE_BLOCK, side='right'),
                          N_EXPERTS - 1)
    xb = xt[buf_tok].reshape(n_blocks, MOE_BLOCK, D_MODEL)

    def expert_block(args):
        xblk, e = args
        return (jax.nn.silu(xblk @ w_gate[e]) * (xblk @ w_up[e])) @ w_down[e]

    yb = lax.map(expert_block, (xb, block_e)).reshape(-1, D_MODEL)
    contrib = yb[slot] * p_sorted[:, None].astype(yb.dtype)
    y = jnp.zeros_like(xt).at[tok_sorted].add(contrib)
    return y.reshape(lead + (D_MODEL,))


def setup_inputs(seed: int = 0) -> dict:
    key = jax.random.key(seed)
    ks = list(jax.random.split(key, 64))
    f32 = jnp.float32
    nrm = lambda shape, scale: jax.random.normal(ks.pop(), shape, f32) * scale
    gain = lambda n: 1.0 + nrm((n,), 0.05)
    n_pages = PAST_LEN // PAGE_SIZE
    n_used = DEC_BATCH * n_pages
    n_pool = n_used + max(1, n_used // 4)
    page_table = jax.random.permutation(ks.pop(), n_pool)[:n_used].reshape(DEC_BATCH, n_pages).astype(jnp.int32)
    return {
        'x_prompt': nrm((BATCH, SEQ, D_MODEL), 1.0),
        'x_sample': nrm((DEC_BATCH, DEC_SEQ, D_MODEL), 1.0),
        'cache_l0_k': nrm((n_pool, PAGE_SIZE, B_HEADS, HEAD_DIM), 1.0),
        'cache_l0_v': nrm((n_pool, PAGE_SIZE, B_HEADS, HEAD_DIM), 1.0),
        'state_l0_wkv': nrm((DEC_BATCH, A_HEADS, HEAD_DIM, HEAD_DIM), 0.3),
        'state_l0_shift': nrm((DEC_BATCH, A_COLS), 1.0),
        'state_l1_conv': nrm((DEC_BATCH, CONV_K - 1, D_MODEL), 0.5),
        'page_table': page_table,
        'norm0_mix_g': gain(D_MODEL),
        'w_in0': nrm((D_MODEL, IN0_COLS), D_MODEL ** -0.5),
        'mu_shift': jax.random.uniform(ks.pop(), (A_COLS,), f32),
        'w_lora_up': nrm((LORA_DECAY, A_WIDTH), 0.1 * LORA_DECAY ** -0.5),
        'w0': jax.random.uniform(ks.pop(), (A_WIDTH,), f32, minval=-6.0, maxval=1.0),
        'a_lora_up': nrm((LORA_ICLR, A_WIDTH), 0.1 * LORA_ICLR ** -0.5),
        'a0': nrm((A_WIDTH,), 0.1),
        'g_lora_up': nrm((LORA_GATE, A_WIDTH), LORA_GATE ** -0.5),
        'k_k': 0.85 + nrm((A_WIDTH,), 0.05),
        'k_a': gain(A_WIDTH),
        'r_k': nrm((A_HEADS, HEAD_DIM), 0.1),
        'lnx_g': gain(A_WIDTH),
        'lnx_b': nrm((A_WIDTH,), 0.01),
        'q_norm_g': gain(HEAD_DIM),
        'k_norm_g': gain(HEAD_DIM),
        'w_out0': nrm((MIX_WIDTH, D_MODEL), MIX_WIDTH ** -0.5),
        'norm0_ffn_g': gain(D_MODEL),
        'ffn_w_gate': nrm((D_MODEL, D_FF), D_MODEL ** -0.5),
        'ffn_w_up': nrm((D_MODEL, D_FF), D_MODEL ** -0.5),
        'ffn_w_down': nrm((D_FF, D_MODEL), D_FF ** -0.5),
        'norm1_mix_g': gain(D_MODEL),
        'conv_w_in': nrm((D_MODEL, 2 * D_MODEL), D_MODEL ** -0.5),
        'conv_dw_w': nrm((CONV_K, D_MODEL), CONV_K ** -0.5),
        'conv_dw_b': nrm((D_MODEL,), 0.01),
        'conv_ln_g': gain(D_MODEL),
        'conv_ln_b': nrm((D_MODEL,), 0.01),
        'conv_w_out': nrm((D_MODEL, D_MODEL), D_MODEL ** -0.5),
        'norm1_ffn_g': gain(D_MODEL),
        'moe_router': nrm((D_MODEL, N_EXPERTS), D_MODEL ** -0.5),
        'moe_w_gate': nrm((N_EXPERTS, D_MODEL, D_FF_EXPERT), D_MODEL ** -0.5),
        'moe_w_up': nrm((N_EXPERTS, D_MODEL, D_FF_EXPERT), D_MODEL ** -0.5),
        'moe_w_down': nrm((N_EXPERTS, D_FF_EXPERT, D_MODEL), D_FF_EXPERT ** -0.5),
    }


def reference(x_prompt, x_sample, cache_l0_k, cache_l0_v, state_l0_wkv, state_l0_shift, state_l1_conv, page_table,
              norm0_mix_g, w_in0, mu_shift, w_lora_up, w0, a_lora_up, a0, g_lora_up, k_k, k_a, r_k, lnx_g, lnx_b,
              q_norm_g, k_norm_g, w_out0, norm0_ffn_g, ffn_w_gate, ffn_w_up, ffn_w_down,
              norm1_mix_g, conv_w_in, conv_dw_w, conv_dw_b, conv_ln_g, conv_ln_b, conv_w_out,
              norm1_ffn_g, moe_router, moe_w_gate, moe_w_up, moe_w_down):
    slopes = alibi_slopes()
    rwkv_p = (mu_shift, w_lora_up, w0, a_lora_up, a0, g_lora_up, k_k, k_a, r_k, lnx_g, lnx_b)
    n_dec, t_dec, _ = x_sample.shape
    past_len = page_table.shape[1] * PAGE_SIZE
    pos_sample = past_len + jnp.arange(t_dec, dtype=jnp.int32)
    xp, xs = x_prompt, x_sample
    for layer in range(DEPTH):
        if layer % 2 == 0:
            k_past = cache_l0_k[page_table].reshape(n_dec, past_len, B_HEADS, HEAD_DIM)
            v_past = cache_l0_v[page_table].reshape(n_dec, past_len, B_HEADS, HEAD_DIM)
            zero_shift = jnp.zeros((xp.shape[0], A_COLS), xp.dtype)
            zero_wkv = jnp.zeros((xp.shape[0], A_HEADS, HEAD_DIM, HEAD_DIM), jnp.float32)
            o_p, k_prompt, v_prompt, wkv_prompt, shift_prompt = layer0_mixer(
                rms_norm(xp, norm0_mix_g), zero_shift, zero_wkv, None, None, slopes,
                w_in0, rwkv_p, q_norm_g, k_norm_g, w_out0)
            o_s, k_sample, v_sample, wkv_sample, shift_sample = layer0_mixer(
                rms_norm(xs, norm0_mix_g), state_l0_shift, state_l0_wkv, (k_past, v_past), pos_sample, slopes,
                w_in0, rwkv_p, q_norm_g, k_norm_g, w_out0)
            xp = xp + o_p
            xs = xs + o_s
            xp = xp + swiglu(rms_norm(xp, norm0_ffn_g), ffn_w_gate, ffn_w_up, ffn_w_down)
            xs = xs + swiglu(rms_norm(xs, norm0_ffn_g), ffn_w_gate, ffn_w_up, ffn_w_down)
        else:
            c_p, conv_prompt = conformer_conv(rms_norm(xp, norm1_mix_g), None, conv_w_in, conv_dw_w, conv_dw_b,
                                              conv_ln_g, conv_ln_b, conv_w_out)
            c_s, conv_sample = conformer_conv(rms_norm(xs, norm1_mix_g), state_l1_conv, conv_w_in, conv_dw_w,
                                              conv_dw_b, conv_ln_g, conv_ln_b, conv_w_out)
            xp = xp + c_p
            xs = xs + c_s
            xp = xp + moe_swiglu(rms_norm(xp, norm1_ffn_g), moe_router, moe_w_gate, moe_w_up, moe_w_down)
            xs = xs + moe_swiglu(rms_norm(xs, norm1_ffn_g), moe_router, moe_w_gate, moe_w_up, moe_w_down)
    return (xp, xs, k_prompt, v_prompt, k_sample, v_sample, wkv_prompt, wkv_sample,
            shift_prompt, shift_sample, conv_prompt, conv_sample)
```

```python
import functools

import jax
import jax.numpy as jnp
from jax import lax
from jax.experimental import pallas as pl
from jax.experimental.pallas import tpu as pltpu

F32 = jnp.float32
BF16 = jnp.bfloat16

D_MODEL = 1024
HEAD_DIM = 64
A_WIDTH = 512
A_HEADS = 8
B_WIDTH = 512
B_HEADS = 8
LORA_DECAY = 64
LORA_ICLR = 64
LORA_GATE = 128
A_COLS = 3 * A_WIDTH + LORA_DECAY + LORA_ICLR + LORA_GATE
MOBA_BLOCK = 256
MOBA_TOPK = 3
PAGE_SIZE = 128
ATTN_SCALE = HEAD_DIM ** -0.5
D_FF = 2816
N_EXPERTS = 8
D_FF_EXPERT = 3584
CONV_K = 31
RMS_EPS = 1e-6
GN_EPS = 64e-5
LN_EPS = 1e-5

LANES = 128
VMEM_LIMIT = 56 * 1024 * 1024
RWKV_CHUNK = 64
MOE_ROWS = 512
HALO = 32
HI = lax.Precision.HIGHEST


def _cparams(sem):
    return pltpu.CompilerParams(dimension_semantics=sem, vmem_limit_bytes=VMEM_LIMIT)


def _rms(x, g):
    return x * lax.rsqrt(jnp.mean(x * x, axis=-1, keepdims=True) + RMS_EPS) * g


def _bdot(a, b):
    return jnp.dot(a.astype(BF16), b.astype(BF16), preferred_element_type=F32)


def _dot_nt(a, b, precision=None):
    return lax.dot_general(a, b, (((1,), (1,)), ((), ())), precision=precision,
                           preferred_element_type=F32)


def _seg_sum(x, bd):
    hi = x.astype(BF16)
    lo = (x - hi.astype(F32)).astype(BF16)
    return (jnp.dot(hi, bd, preferred_element_type=F32)
            + jnp.dot(lo, bd, preferred_element_type=F32))


def _block_diag_ones(n, seg):
    i = jnp.arange(n)
    return (i[:, None] // seg == i[None, :] // seg).astype(BF16)


def _norm_proj_kernel(x_ref, g_ref, w_ref, *out_refs, splits):
    h = _rms(x_ref[...], g_ref[...]).astype(BF16)
    for o_ref, (c0, cn) in zip(out_refs, splits):
        o_ref[...] = jnp.dot(h, w_ref[:, c0:c0 + cn], preferred_element_type=F32)


def norm_proj(x, g, w_bf16, splits, tm):
    n, d = x.shape
    cols = w_bf16.shape[1]
    return pl.pallas_call(
        functools.partial(_norm_proj_kernel, splits=splits),
        grid=(n // tm,),
        in_specs=[pl.BlockSpec((tm, d), lambda i: (i, 0)),
                  pl.BlockSpec((1, d), lambda i: (0, 0)),
                  pl.BlockSpec((d, cols), lambda i: (0, 0))],
        out_specs=[pl.BlockSpec((tm, cn), lambda i: (i, 0)) for _, cn in splits],
        out_shape=[jax.ShapeDtypeStruct((n, cn), F32) for _, cn in splits],
        compiler_params=_cparams(("parallel",)),
        name="norm_proj",
    )(x, g.reshape(1, d), w_bf16)


def _rwkv_prep_kernel(p_ref, shift_ref, mu_ref, wl_ref, w0_ref, al_ref, a0_ref, gl_ref,
                      kk_ref, ka_ref, rk_ref, bd_ref,
                      r_o, lw_o, k_o, v_o, kap_o, b_o, g_o, bonus_o, carry):
    t = pl.program_id(1)

    @pl.when(t == 0)
    def _():
        carry[...] = shift_ref[0]

    p = p_ref[0]
    tt = p.shape[0]
    rolled = pltpu.roll(p, shift=1, axis=0)
    row = lax.broadcasted_iota(jnp.int32, p.shape, 0)
    prev = jnp.where(row == 0, carry[...], rolled)
    carry[...] = p[tt - 1:tt, :]
    xs = p + (prev - p) * mu_ref[...]
    aw = A_WIDTH
    r = xs[:, 0:aw]
    k = xs[:, aw:2 * aw]
    v = xs[:, 2 * aw:3 * aw]
    c0 = 3 * aw
    xw = xs[:, c0:c0 + LORA_DECAY]
    xa = xs[:, c0 + LORA_DECAY:c0 + LORA_DECAY + LORA_ICLR]
    xg = xs[:, c0 + LORA_DECAY + LORA_ICLR:]
    bd = bd_ref[...]

    z = w0_ref[...] + _bdot(jnp.tanh(xw), wl_ref[...])
    nz = -z
    softplus = jnp.maximum(nz, 0.0) + jnp.log(1.0 + jnp.exp(-jnp.abs(nz)))
    w_log = -softplus - 0.5
    lw_o[0] = -jnp.exp(w_log)
    a = jax.nn.sigmoid(a0_ref[...] + _bdot(xa, al_ref[...]))
    g_o[0] = _bdot(jax.nn.sigmoid(xg), gl_ref[...])
    kk = k * kk_ref[...]
    nrm = jnp.sqrt(_seg_sum(kk * kk, bd))
    kap = kk / jnp.maximum(nrm, 1e-12)
    k_h = k * (1.0 + (a - 1.0) * ka_ref[...])
    r_o[0] = r
    k_o[0] = k_h
    v_o[0] = v
    kap_o[0] = kap
    b_o[0] = kap * a
    bonus_o[0] = _seg_sum(r * k_h * rk_ref[...], bd) * v


def rwkv_prep(p_a, shift_prev, mu_shift, w_lora_up, w0, a_lora_up, a0, g_lora_up, k_k, k_a, r_k, tt):
    b, t, _ = p_a.shape
    row = lambda x: x.reshape(1, -1)
    const = lambda shape: pl.BlockSpec(shape, lambda i, j: (0,) * len(shape))
    tok = lambda w: pl.BlockSpec((1, tt, w), lambda i, j: (i, j, 0))
    outs = [jax.ShapeDtypeStruct((b, t, A_WIDTH), F32)] * 8
    return pl.pallas_call(
        _rwkv_prep_kernel,
        grid=(b, t // tt),
        in_specs=[tok(A_COLS),
                  pl.BlockSpec((1, 1, A_COLS), lambda i, j: (i, 0, 0)),
                  const((1, A_COLS)),
                  const((LORA_DECAY, A_WIDTH)), const((1, A_WIDTH)),
                  const((LORA_ICLR, A_WIDTH)), const((1, A_WIDTH)),
                  const((LORA_GATE, A_WIDTH)),
                  const((1, A_WIDTH)), const((1, A_WIDTH)), const((1, A_WIDTH)),
                  const((A_WIDTH, A_WIDTH))],
        out_specs=[tok(A_WIDTH)] * 8,
        out_shape=outs,
        scratch_shapes=[pltpu.VMEM((1, A_COLS), F32)],
        compiler_params=_cparams(("parallel", "arbitrary")),
        name="rwkv_prep",
    )(p_a, shift_prev.reshape(b, 1, A_COLS), row(mu_shift), w_lora_up.astype(BF16), row(w0),
      a_lora_up.astype(BF16), row(a0), g_lora_up.astype(BF16), row(k_k), row(k_a), row(r_k),
      _block_diag_ones(A_WIDTH, HEAD_DIM))


def _unit_lower_inverse(l, n, c):
    ri = lax.broadcasted_iota(jnp.int32, (n, n), 0)
    ci = lax.broadcasted_iota(jnp.int32, (n, n), 1)
    eye = (ri == ci).astype(F32)

    def same_block(bs):
        return (ri // bs) == (ci // bs)

    mm = functools.partial(jnp.dot, precision=HI, preferred_element_type=F32)
    b0 = min(16, c)
    neg = jnp.where(same_block(b0), -l, 0.0)
    inv = eye + neg
    power, span = neg, 2
    while span < b0:
        power = mm(power, power)
        inv = inv + mm(inv, power)
        span *= 2
    bs = b0
    while bs < c:
        cross = jnp.where(same_block(2 * bs) & jnp.logical_not(same_block(bs)), l, 0.0)
        inv = inv - mm(mm(inv, cross), inv)
        bs *= 2
    return inv


def _rwkv_chunk_kernel(r_ref, lw_ref, k_ref, v_ref, kap_ref, b_ref, s0_ref, y_ref, sout_ref, state,
                       *, chunk, t_valid):
    ci = pl.program_id(2)
    c = chunk
    n = 2 * c
    hd = HEAD_DIM

    @pl.when(ci == 0)
    def _():
        s0 = s0_ref[0]
        z = jnp.zeros((hd, hd), F32)
        state[...] = jnp.concatenate(
            [jnp.concatenate([s0[0], z], axis=1), jnp.concatenate([z, s0[1]], axis=1)], axis=0)

    mm = functools.partial(jnp.dot, precision=HI, preferred_element_type=F32)
    lane = lax.broadcasted_iota(jnp.int32, (c, LANES), 1)
    trow = lax.broadcasted_iota(jnp.int32, (c, LANES), 0) + ci * c
    live = trow < t_valid
    zero = jnp.zeros((c, LANES), F32)
    r = r_ref[0]
    lw = jnp.where(live, lw_ref[0], zero)
    k = jnp.where(live, k_ref[0], zero)
    v = jnp.where(live, v_ref[0], zero)
    kap = jnp.where(live, kap_ref[0], zero)
    b = jnp.where(live, b_ref[0], zero)

    tri = (lax.broadcasted_iota(jnp.int32, (c, c), 0)
           >= lax.broadcasted_iota(jnp.int32, (c, c), 1)).astype(F32)
    cum = mm(tri, lw)
    total = cum[c - 1:c, :]
    gam = jnp.exp(cum)
    ginv = jnp.exp(-cum)
    kap_h = kap * jnp.exp(cum - lw)
    b_h = b * ginv
    k_h = k * ginv
    r_h = r * gam
    tail = jnp.exp(total - cum)
    k_bar = k * tail
    b_bar = b * tail

    m0 = lane < hd

    def stack(x):
        return jnp.concatenate([jnp.where(m0, x, 0.0), jnp.where(m0, 0.0, x)], axis=0)

    kap_s, r_s, b_s, k_s, v_s = stack(kap_h), stack(r_h), stack(b_h), stack(k_h), stack(v)
    kbar_s, bbar_s = stack(k_bar), stack(b_bar)

    ri = lax.broadcasted_iota(jnp.int32, (n, n), 0)
    cj = lax.broadcasted_iota(jnp.int32, (n, n), 1)
    same = (ri // c) == (cj // c)
    strict = same & (ri > cj)
    incl = same & (ri >= cj)
    l_b = jnp.where(strict, _dot_nt(kap_s, b_s, HI), 0.0)
    l_k = jnp.where(strict, _dot_nt(kap_s, k_s, HI), 0.0)
    a_qk = jnp.where(incl, _dot_nt(r_s, k_s, HI), 0.0)
    a_qb = jnp.where(incl, _dot_nt(r_s, b_s, HI), 0.0)
    inv = _unit_lower_inverse(l_b, n, c)
    w_s = mm(inv, kap_s)
    u0_s = mm(inv, mm(l_k, v_s))

    s = state[...]
    u_s = _dot_nt(w_s, s, HI) + u0_s
    y_s = _dot_nt(r_s, s, HI) + mm(a_qk, v_s) - mm(a_qb, u_s)
    y_ref[0] = y_s[0:c] + y_s[c:n]
    s_new = s * jnp.exp(total) +mm(v_s.T, kbar_s) - mm(u_s.T, bbar_s)
    state[...] = s_new

    @pl.when(ci == pl.num_programs(2) - 1)
    def _():
        sout_ref[0] = jnp.stack([s_new[0:hd, 0:hd], s_new[hd:2 * hd, hd:2 * hd]], axis=0)


def rwkv_chunked(r, lw, k, v, kap, b, s0, chunk, t_valid):
    bsz, t, _ = r.shape
    hp = A_HEADS // 2
    tok = pl.BlockSpec((1, chunk, LANES), lambda i, j, c: (i, c, j))
    st = pl.BlockSpec((1, 2, HEAD_DIM, HEAD_DIM), lambda i, j, c: (i, j, 0, 0))
    y, s_out = pl.pallas_call(
        functools.partial(_rwkv_chunk_kernel, chunk=chunk, t_valid=t_valid),
        grid=(bsz, hp, t // chunk),
        in_specs=[tok] * 6 + [st],
        out_specs=[tok, st],
        out_shape=[jax.ShapeDtypeStruct((bsz, t, A_WIDTH), F32),
                   jax.ShapeDtypeStruct((bsz, A_HEADS, HEAD_DIM, HEAD_DIM), F32)],
        scratch_shapes=[pltpu.VMEM((LANES, LANES), F32)],
        compiler_params=_cparams(("parallel", "parallel", "arbitrary")),
        name="rwkv_chunk",
    )(r, lw, k, v, kap, b, s0)
    return y, s_out


def _mix_out_kernel(x_ref, y_ref, bonus_ref, g_ref, o_ref, lng_ref, lnb_ref, bd_ref, w_ref, out_ref):
    y = y_ref[...]
    bd = bd_ref[...]
    mu = _seg_sum(y, bd) * (1.0 / HEAD_DIM)
    d = y - mu
    var = _seg_sum(d * d, bd) * (1.0 / HEAD_DIM)
    yn = d * lax.rsqrt(var + GN_EPS) * lng_ref[...] + lnb_ref[...]
    ya = (yn + bonus_ref[...]) * g_ref[...]
    acc = jnp.dot(ya.astype(BF16), w_ref[0:A_WIDTH, :], preferred_element_type=F32)
    acc += jnp.dot(o_ref[...].astype(BF16), w_ref[A_WIDTH:, :], preferred_element_type=F32)
    out_ref[...] = x_ref[...] + acc


def mix_out(x, y, bonus, g, o, lnx_g, lnx_b, w_out_bf16, tm):
    n, d = x.shape
    tok = lambda w: pl.BlockSpec((tm, w), lambda i: (i, 0))
    const = lambda shape: pl.BlockSpec(shape, lambda i: (0,) * len(shape))
    return pl.pallas_call(
        _mix_out_kernel,
        grid=(n // tm,),
        in_specs=[tok(d), tok(A_WIDTH), tok(A_WIDTH), tok(A_WIDTH), tok(B_WIDTH),
                  const((1, A_WIDTH)), const((1, A_WIDTH)), const((A_WIDTH, A_WIDTH)),
                  const((A_WIDTH + B_WIDTH, d))],
        out_specs=tok(d),
        out_shape=jax.ShapeDtypeStruct((n, d), F32),
        compiler_params=_cparams(("parallel",)),
        name="mix_out",
    )(x, y, bonus, g, o, lnx_g.reshape(1, -1), lnx_b.reshape(1, -1),
      _block_diag_ones(A_WIDTH, HEAD_DIM), w_out_bf16)


def _ffn_kernel(x_ref, g_ref, wg_ref, wu_ref, wd_ref, out_ref, h_scr, acc):
    j = pl.program_id(1)

    @pl.when(j == 0)
    def _():
        h_scr[...] = _rms(x_ref[...], g_ref[...]).astype(BF16)
        acc[...] = jnp.zeros_like(acc)

    h = h_scr[...]
    gate = jnp.dot(h, wg_ref[...], preferred_element_type=F32)
    up = jnp.dot(h, wu_ref[...], preferred_element_type=F32)
    act = (jax.nn.silu(gate) * up).astype(BF16)
    acc[...] += jnp.dot(act, wd_ref[...], preferred_element_type=F32)

    @pl.when(j == pl.num_programs(1) - 1)
    def _():
        out_ref[...] = x_ref[...] + acc[...]


def ffn(x, g, wg, wu, wd, tm, tf):
    n, d = x.shape
    f = wg.shape[1]
    return pl.pallas_call(
        _ffn_kernel,
        grid=(n // tm, f // tf),
        in_specs=[pl.BlockSpec((tm, d), lambda i, j: (i, 0)),
                  pl.BlockSpec((1, d), lambda i, j: (0, 0)),
                  pl.BlockSpec((d, tf), lambda i, j: (0, j)),
                  pl.BlockSpec((d, tf), lambda i, j: (0, j)),
                  pl.BlockSpec((tf, d), lambda i, j: (j, 0))],
        out_specs=pl.BlockSpec((tm, d), lambda i, j: (i, 0)),
        out_shape=jax.ShapeDtypeStruct((n, d), F32),
        scratch_shapes=[pltpu.VMEM((tm, d), BF16), pltpu.VMEM((tm, d), F32)],
        compiler_params=_cparams(("parallel", "arbitrary")),
        name="ffn",
    )(x, g.reshape(1, d), wg, wu, wd)


def _conv_in_kernel(x_ref, g_ref, w_ref, u_ref):
    h = _rms(x_ref[...], g_ref[...]).astype(BF16)
    d = u_ref.shape[1]
    a = jnp.dot(h, w_ref[:, 0:d], preferred_element_type=F32)
    gate = jnp.dot(h, w_ref[:, d:2 * d], preferred_element_type=F32)
    u_ref[...] = a * jax.nn.sigmoid(gate)


def conv_in(x, g, w_bf16, tm):
    n, d = x.shape
    return pl.pallas_call(
        _conv_in_kernel,
        grid=(n // tm,),
        in_specs=[pl.BlockSpec((tm, d), lambda i: (i, 0)),
                  pl.BlockSpec((1, d), lambda i: (0, 0)),
                  pl.BlockSpec((d, 2 * d), lambda i: (0, 0))],
        out_specs=pl.BlockSpec((tm, d), lambda i: (i, 0)),
        out_shape=jax.ShapeDtypeStruct((n, d), F32),
        compiler_params=_cparams(("parallel",)),
        name="conv_in",
    )(x, g.reshape(1, d), w_bf16)


def _conv_dw_kernel(u_ref, init_ref, x_ref, dw_ref, db_ref, lg_ref, lb_ref, w_ref, out_ref, ctx):
    t = pl.program_id(1)
    tt = u_ref.shape[1]

    @pl.when(t == 0)
    def _():
        ctx[0:HALO, :] = init_ref[0]

    ctx[HALO:HALO + tt, :] = u_ref[0]
    acc = jnp.zeros((tt, D_MODEL), F32) + db_ref[...]
    for j in range(CONV_K):
        acc = acc + ctx[pl.ds(HALO - (CONV_K - 1) + j, tt), :] * dw_ref[j:j + 1, :]
    keep = ctx[tt:tt + HALO, :]
    ctx[0:HALO, :] = keep
    mu = jnp.mean(acc, axis=-1, keepdims=True)
    d = acc - mu
    var = jnp.mean(d * d, axis=-1, keepdims=True)
    cf = d * lax.rsqrt(var + LN_EPS) * lg_ref[...] + lb_ref[...]
    out_ref[0] = x_ref[0] + jnp.dot(jax.nn.silu(cf).astype(BF16), w_ref[...], preferred_element_type=F32)


def conv_dw(u, init, x, dw_w, dw_b, ln_g, ln_b, w_out_bf16, tt):
    b, t, d = u.shape
    dw = jnp.pad(dw_w, ((0, HALO - CONV_K), (0, 0)))
    tok = pl.BlockSpec((1, tt, d), lambda i, j: (i, j, 0))
    const = lambda shape: pl.BlockSpec(shape, lambda i, j: (0,) * len(shape))
    return pl.pallas_call(
        _conv_dw_kernel,
        grid=(b, t // tt),
        in_specs=[tok, pl.BlockSpec((1, HALO, d), lambda i, j: (i, 0, 0)), tok,
                  const((HALO, d)), const((1, d)), const((1, d)), const((1, d)), const((d, d))],
        out_specs=tok,
        out_shape=jax.ShapeDtypeStruct((b, t, d), F32),
        scratch_shapes=[pltpu.VMEM((HALO + tt, d), F32)],
        compiler_params=_cparams(("parallel", "arbitrary")),
        name="conv_dw",
    )(u, init, x, dw, dw_b.reshape(1, d), ln_g.reshape(1, d), ln_b.reshape(1, d), w_out_bf16)


def _router_kernel(x_ref, g_ref, wr_ref, idx_ref, pw_ref):
    h = _rms(x_ref[...], g_ref[...])
    logits = _bdot(h, wr_ref[...])
    lane = lax.broadcasted_iota(jnp.int32, logits.shape, 1)
    valid = lane < N_EXPERTS
    logits = jnp.where(valid, logits, -jnp.inf)
    e = jnp.exp(logits - jnp.max(logits, axis=-1, keepdims=True))
    probs = jnp.where(valid, e / jnp.sum(e, axis=-1, keepdims=True), -1.0)
    m1 = jnp.max(probs, axis=-1, keepdims=True)
    i1 = jnp.min(jnp.where(probs == m1, lane, LANES), axis=-1, keepdims=True)
    rest = jnp.where(lane == i1, -1.0, probs)
    m2 = jnp.max(rest, axis=-1, keepdims=True)
    i2 = jnp.min(jnp.where(rest == m2, lane, LANES), axis=-1, keepdims=True)
    den = m1 + m2
    idx_ref[...] = jnp.where(lane == 0, i1, jnp.where(lane == 1, i2, 0))
    pw_ref[...] = jnp.where(lane == 0, m1 / den, jnp.where(lane == 1, m2 / den, 0.0))


def router(x, g, w_router, tm):
    n, d = x.shape
    wr = jnp.pad(w_router, ((0, 0), (0, LANES - N_EXPERTS))).astype(BF16)
    return pl.pallas_call(
        _router_kernel,
        grid=(n // tm,),
        in_specs=[pl.BlockSpec((tm, d), lambda i: (i, 0)),
                  pl.BlockSpec((1, d), lambda i: (0, 0)),
                  pl.BlockSpec((d, LANES), lambda i: (0, 0))],
        out_specs=[pl.BlockSpec((tm, LANES), lambda i: (i, 0))] * 2,
        out_shape=[jax.ShapeDtypeStruct((n, LANES), jnp.int32), jax.ShapeDtypeStruct((n, LANES), F32)],
        compiler_params=_cparams(("parallel",)),
        name="moe_router",
    )(x, g.reshape(1, d), wr)


def _moe_kernel(be_ref, nu_ref, tok_ref, dst_ref,
                x_hbm, g_ref, p_ref, wg_ref, wu_ref, wd_ref,
                y_hbm, xg, h_scr, acc, sem_in, sem_out, *, n_dst):
    i = pl.program_id(0)
    j = pl.program_id(1)
    m = xg.shape[0]
    valid = i < nu_ref[0]
    base = i * m

    def row_in(r, tok):
        return pltpu.make_async_copy(x_hbm.at[pl.ds(tok, 1), :], xg.at[pl.ds(r, 1), :], sem_in)

    def row_out(r, dst):
        return pltpu.make_async_copy(xg.at[pl.ds(r, 1), :], y_hbm.at[pl.ds(dst, 1), :], sem_out)

    @pl.when(valid & (j == 0))
    def _():
        def start(r, c):
            row_in(r, tok_ref[base + r]).start()
            return c

        def wait(r, c):
            row_in(r, 0).wait()
            return c

        lax.fori_loop(0, m, start, 0)
        lax.fori_loop(0, m, wait, 0)
        h_scr[...] = _rms(xg[...], g_ref[...]).astype(BF16)
        acc[...] = jnp.zeros_like(acc)

    @pl.when(valid)
    def _():
        h = h_scr[...]
        gate = jnp.dot(h, wg_ref[0], preferred_element_type=F32)
        up = jnp.dot(h, wu_ref[0], preferred_element_type=F32)
        act = (jax.nn.silu(gate) * up).astype(BF16)
        acc[...] += jnp.dot(act, wd_ref[0], preferred_element_type=F32)

    @pl.when(valid & (j == pl.num_programs(1) - 1))
    def _():
        xg[...] = acc[...] * p_ref[...]

        def start(r, c):
            dst = dst_ref[base + r]

            @pl.when(dst < n_dst)
            def _():
                row_out(r, dst).start()
            return c

        def wait(r, c):
            dst = dst_ref[base + r]

            @pl.when(dst < n_dst)
            def _():
                row_out(r, 0).wait()
            return c

        lax.fori_loop(0, m, start, 0)
        lax.fori_loop(0, m, wait, 0)


def moe_experts(x, g, block_e, n_used, buf_tok, buf_dst, buf_p, wg, wu, wd, m, tf):
    n, d = x.shape
    n_blocks = block_e.shape[0]
    nj = D_FF_EXPERT // tf

    def jj(i, j, nu):
        return jnp.where(i < nu[0], j, nj - 1)

    grid_spec = pltpu.PrefetchScalarGridSpec(
        num_scalar_prefetch=4,
        grid=(n_blocks, nj),
        in_specs=[pl.BlockSpec(memory_space=pl.ANY),
                  pl.BlockSpec((1, d), lambda i, j, be, nu, tk, ds: (0, 0)),
                  pl.BlockSpec((m, 1), lambda i, j, be, nu, tk, ds: (i, 0)),
                  pl.BlockSpec((1, d, tf), lambda i, j, be, nu, tk, ds: (be[i], 0, jj(i, j, nu))),
                  pl.BlockSpec((1, d, tf), lambda i, j, be, nu, tk, ds: (be[i], 0, jj(i, j, nu))),
                  pl.BlockSpec((1, tf, d), lambda i, j, be, nu, tk, ds: (be[i], jj(i, j, nu), 0))],
        out_specs=pl.BlockSpec(memory_space=pl.ANY),
        scratch_shapes=[pltpu.VMEM((m, d), F32), pltpu.VMEM((m, d), BF16), pltpu.VMEM((m, d), F32),
                        pltpu.SemaphoreType.DMA, pltpu.SemaphoreType.DMA],
    )
    return pl.pallas_call(
        functools.partial(_moe_kernel, n_dst=2 * n),
        grid_spec=grid_spec,
        out_shape=jax.ShapeDtypeStruct((2 * n, d), F32),
        compiler_params=_cparams(("arbitrary", "arbitrary")),
        name="moe_experts",
    )(block_e, n_used, buf_tok, buf_dst, x, g.reshape(1, d), buf_p, wg, wu, wd)


def _combine_kernel(x_ref, a_ref, b_ref, o_ref):
    o_ref[...] = x_ref[...] + (a_ref[0] + b_ref[0])


def moe_combine(x, y2, tm):
    n, d = x.shape
    y3 = y2.reshape(2, n, d)
    return pl.pallas_call(
        _combine_kernel,
        grid=(n // tm,),
        in_specs=[pl.BlockSpec((tm, d), lambda i: (i, 0)),
                  pl.BlockSpec((1, tm, d), lambda i: (0, i, 0)),
                  pl.BlockSpec((1, tm, d), lambda i: (1, i, 0))],
        out_specs=pl.BlockSpec((tm, d), lambda i: (i, 0)),
        out_shape=jax.ShapeDtypeStruct((n, d), F32),
        compiler_params=_cparams(("parallel",)),
        name="moe_combine",
    )(x, y3, y3)


def moe_layer(x, g, w_router, wg, wu, wd, m):
    n, d = x.shape
    idx, pw = router(x, g, w_router, min(n, 512))
    flat_e = idx[:, :2].reshape(-1)
    flat_p = pw[:, :2].reshape(-1)
    onehot = (flat_e[:, None] == jnp.arange(N_EXPERTS, dtype=jnp.int32)[None, :]).astype(jnp.int32)
    rank = jnp.sum((jnp.cumsum(onehot, axis=0) - onehot) * onehot, axis=1)
    counts = jnp.sum(onehot, axis=0)
    padded = (counts + m - 1) // m * m
    pend = jnp.cumsum(padded)
    slot = (pend - padded)[flat_e] + rank
    n_blocks = (2 * n) // m + N_EXPERTS
    n_slots = n_blocks * m
    pair = jnp.arange(2 * n, dtype=jnp.int32)
    buf_tok = jnp.zeros((n_slots,), jnp.int32).at[slot].set(pair // 2)
    buf_dst = jnp.full((n_slots,), 2 * n, jnp.int32).at[slot].set((pair % 2) * n + pair // 2)
    buf_p = jnp.zeros((n_slots,), F32).at[slot].set(flat_p).reshape(n_slots, 1)
    block_e = jnp.minimum(jnp.searchsorted(pend, jnp.arange(n_blocks, dtype=jnp.int32) * m, side='right'),
                          N_EXPERTS - 1).astype(jnp.int32)
    n_used = (pend[-1] // m).astype(jnp.int32).reshape(1)
    y2 = moe_experts(x, g, block_e, n_used, buf_tok, buf_dst, buf_p, wg, wu, wd, m, 512)
    return moe_combine(x, y2, min(n, 512))


def _moba_prep_kernel(q_ref, k_ref, qg_ref, kg_ref, bd_ref, qn_ref, kn_ref, km_ref):
    bd = bd_ref[...]

    def head_norm(x, g):
        ms = _seg_sum(x * x, bd) * (1.0 / HEAD_DIM)
        return x * lax.rsqrt(ms + RMS_EPS) * g

    qn_ref[...] = head_norm(q_ref[...], qg_ref[...])
    kn = head_norm(k_ref[...], kg_ref[...])
    kn_ref[...] = kn
    km_ref[0] = jnp.mean(kn, axis=0, keepdims=True)


def moba_prep(q, k, q_norm_g, k_norm_g, tm):
    n, w = q.shape
    tok = pl.BlockSpec((tm, w), lambda i: (i, 0))
    const = lambda shape: pl.BlockSpec(shape, lambda i: (0,) * len(shape))
    return pl.pallas_call(
        _moba_prep_kernel,
        grid=(n // tm,),
        in_specs=[tok, tok, const((1, w)), const((1, w)), const((w, w))],
        out_specs=[tok, tok, pl.BlockSpec((1, 1, w), lambda i: (i, 0, 0))],
        out_shape=[jax.ShapeDtypeStruct((n, w), F32), jax.ShapeDtypeStruct((n, w), F32),
                   jax.ShapeDtypeStruct((n // tm, 1, w), F32)],
        compiler_params=_cparams(("parallel",)),
        name="moba_prep",
    )(q, k, jnp.tile(q_norm_g, B_HEADS).reshape(1, w), jnp.tile(k_norm_g, B_HEADS).reshape(1, w),
      _block_diag_ones(w, HEAD_DIM))


def _top_blocks(gate, bidx, nb):
    chosen = jnp.zeros(gate.shape, F32)
    for _ in range(MOBA_TOPK):
        mx = jnp.max(gate, axis=-1, keepdims=True)
        first = jnp.min(jnp.where(gate == mx, bidx, nb), axis=-1, keepdims=True)
        chosen = jnp.where((bidx == first) & (mx > -jnp.inf), 1.0, chosen)
        gate = jnp.where(bidx == first, -jnp.inf, gate)
    return chosen


def _moba_attn_kernel(qi_ref, ki_ref, q_ref, k_ref, v_ref, km_ref, o_ref, sel, m_scr, l_scr, acc):
    hp = pl.program_id(1)
    s = pl.program_id(2)
    qi = qi_ref[s]
    ki = ki_ref[s]
    blk = MOBA_BLOCK
    nb = km_ref.shape[1]
    q = q_ref[0]
    k = k_ref[0].astype(BF16)
    v = v_ref[0].astype(BF16)
    lane = lax.broadcasted_iota(jnp.int32, (blk, LANES), 1)
    rel = (lax.broadcasted_iota(jnp.int32, (blk, blk), 0)
           - lax.broadcasted_iota(jnp.int32, (blk, blk), 1))
    bidx = lax.broadcasted_iota(jnp.int32, (blk, nb), 1)
    is_own = ki == qi
    dist = (rel + (qi - ki) * blk).astype(F32)

    for h in range(2):
        in_head = (lane < HEAD_DIM) if h == 0 else (lane >= HEAD_DIM)
        qh = jnp.where(in_head, q, 0.0).astype(BF16)

        @pl.when(is_own)
        def _():
            gate = _dot_nt(qh, km_ref[0].astype(BF16))
            sel[h] = _top_blocks(jnp.where(bidx < qi, gate, -jnp.inf), bidx, nb)
            m_scr[h] = jnp.full((blk, 1), -jnp.inf, F32)
            l_scr[h] = jnp.zeros((blk, 1), F32)
            acc[h] = jnp.zeros((blk, LANES), F32)

        slope = jnp.exp2(-jnp.full((1, 1), 2 * hp + h + 1, jnp.int32).astype(F32))
        sc = _dot_nt(qh, k) * ATTN_SCALE - slope * dist
        picked = jnp.sum(jnp.where(bidx == ki, sel[h], 0.0), axis=-1, keepdims=True) > 0.5
        lim = jnp.where(is_own, 0, jnp.where(picked, -blk, blk))
        sc = jnp.where(rel >= lim, sc, -jnp.inf)
        m_prev = m_scr[h]
        m_new = jnp.maximum(m_prev, jnp.max(sc, axis=-1, keepdims=True))
        alpha = jnp.exp(m_prev - m_new)
        p = jnp.exp(sc - m_new)
        l_scr[h] = alpha * l_scr[h] + jnp.sum(p, axis=-1, keepdims=True)
        acc[h] = alpha * acc[h] + jnp.dot(p.astype(BF16), v, preferred_element_type=F32)
        m_scr[h] = m_new

    @pl.when((ki == qi - 1) | (qi == 0))
    def _():
        o_ref[0] = jnp.where(lane < HEAD_DIM, acc[0] / l_scr[0], acc[1] / l_scr[1])


def moba_prefill(qn, kn, v, km):
    b, t, w = qn.shape
    blk = MOBA_BLOCK
    nb = t // blk
    pairs = [(qi, ki) for qi in range(nb) for ki in [qi] + list(range(qi))]
    qi_tab = jnp.array([p[0] for p in pairs], jnp.int32)
    ki_tab = jnp.array([p[1] for p in pairs], jnp.int32)
    grid_spec = pltpu.PrefetchScalarGridSpec(
        num_scalar_prefetch=2,
        grid=(b, w // LANES, len(pairs)),
        in_specs=[pl.BlockSpec((1, blk, LANES), lambda i, j, s, qt, kt: (i, qt[s], j)),
                  pl.BlockSpec((1, blk, LANES), lambda i, j, s, qt, kt: (i, kt[s], j)),
                  pl.BlockSpec((1, blk, LANES), lambda i, j, s, qt, kt: (i, kt[s], j)),
                  pl.BlockSpec((1, nb, LANES), lambda i, j, s, qt, kt: (i, 0, j))],
        out_specs=pl.BlockSpec((1, blk, LANES), lambda i, j, s, qt, kt: (i, qt[s], j)),
        scratch_shapes=[pltpu.VMEM((2, blk, nb), F32), pltpu.VMEM((2, blk, 1), F32),
                        pltpu.VMEM((2, blk, 1), F32), pltpu.VMEM((2, blk, LANES), F32)],
    )
    return pl.pallas_call(
        _moba_attn_kernel,
        grid_spec=grid_spec,
        out_shape=jax.ShapeDtypeStruct((b, t, w), F32),
        compiler_params=_cparams(("parallel", "parallel", "arbitrary")),
        name="moba_prefill",
    )(qi_tab, ki_tab, qn, kn, v, km)


DEC_TOK = 8
DEC_PAGES = 8


def _moba_dec_kernel(pt_ref, q_ref, kn_ref, vn_ref, *refs, t_new, past_len):
    npg = DEC_PAGES
    k_refs, v_refs = refs[:npg], refs[npg:2 * npg]
    o_ref, km_scr, sel, m_scr, l_scr, acc = refs[2 * npg:]
    ph = pl.program_id(1)
    g = pl.program_id(2)
    last_g = g == pl.num_programs(2) - 1
    rows = B_HEADS * DEC_TOK
    w = B_WIDTH
    nb = km_scr.shape[0]
    per_blk = MOBA_BLOCK // PAGE_SIZE

    qt = jnp.concatenate([q_ref[0]] * B_HEADS, axis=0)
    head_of_lane = lax.broadcasted_iota(jnp.int32, (rows, w), 1) // HEAD_DIM
    head_of_row = lax.broadcasted_iota(jnp.int32, (rows, w), 0) // DEC_TOK
    own_lanes = head_of_lane == head_of_row
    qs = jnp.where(own_lanes, qt, 0.0).astype(BF16)
    rcol = lax.broadcasted_iota(jnp.int32, (rows, 1), 0)
    tok = rcol % DEC_TOK
    slope = jnp.exp2(-(rcol // DEC_TOK + 1).astype(F32))
    bidx = lax.broadcasted_iota(jnp.int32, (rows, nb), 1)

    def update(sc, vals):
        m_prev = m_scr[...]
        m_new = jnp.maximum(m_prev, jnp.max(sc, axis=-1, keepdims=True))
        alpha = jnp.exp(m_prev - m_new)
        p = jnp.exp(sc - m_new)
        l_scr[...] = alpha * l_scr[...] + jnp.sum(p, axis=-1, keepdims=True)
        acc[...] = alpha * acc[...] + jnp.dot(p.astype(BF16), vals, preferred_element_type=F32)
        m_scr[...] = m_new

    @pl.when(ph == 0)
    def _():
        for i in range(0, npg, per_blk):
            tot = jnp.sum(k_refs[i][0], axis=0, keepdims=True)
            for u in range(1, per_blk):
                tot = tot + jnp.sum(k_refs[i + u][0], axis=0, keepdims=True)
            km_scr[pl.ds(g * (npg // per_blk) + i // per_blk, 1), :] = tot * (1.0 / MOBA_BLOCK)

        @pl.when(last_g)
        def _():
            gate = _dot_nt(qs, km_scr[...].astype(BF16))
            sel[...] = _top_blocks(gate, bidx, nb)
            c = lax.broadcasted_iota(jnp.int32, (rows, DEC_TOK), 1)
            sc = (_dot_nt(qs, kn_ref[0].astype(BF16)) * ATTN_SCALE
                  - slope * (tok - c).astype(F32))
            sc = jnp.where((c <= tok) & (c < t_new), sc, -jnp.inf)
            m_scr[...] = jnp.full((rows, 1), -jnp.inf, F32)
            l_scr[...] = jnp.zeros((rows, 1), F32)
            acc[...] = jnp.zeros((rows, w), F32)
            update(sc, vn_ref[0].astype(BF16))

    @pl.when(ph == 1)
    def _():
        col = lax.broadcasted_iota(jnp.int32, (rows, PAGE_SIZE), 1)
        for i in range(npg):
            page = g * npg + i
            dist = (past_len + tok) - (page * PAGE_SIZE + col)
            sc = (_dot_nt(qs, k_refs[i][0].astype(BF16)) * ATTN_SCALE
                  - slope * dist.astype(F32))
            picked = jnp.sum(jnp.where(bidx == page // per_blk, sel[...], 0.0),
                             axis=-1, keepdims=True) > 0.5
            update(jnp.where(picked, sc, -jnp.inf), v_refs[i][0].astype(BF16))

        @pl.when(last_g)
        def _():
            o = jnp.where(own_lanes, acc[...] / l_scr[...], 0.0)
            o_ref[0] = jnp.sum(o.reshape(B_HEADS, DEC_TOK, w), axis=0)


def moba_decode(qn, kn, vn, cache_k, cache_v, page_table, t_new):
    s, _, w = qn.shape
    n_pages = page_table.shape[1]
    past_len = n_pages * PAGE_SIZE
    assert past_len % MOBA_BLOCK == 0 and n_pages % DEC_PAGES == 0 and t_new <= DEC_TOK
    nb = past_len // MOBA_BLOCK
    rows = B_HEADS * DEC_TOK
    new = pl.BlockSpec((1, DEC_TOK, w), lambda b, ph, g, pt: (b, 0, 0))

    def k_spec(i):
        return pl.BlockSpec((1, PAGE_SIZE, w), lambda b, ph, g, pt: (pt[b, g * DEC_PAGES + i], 0, 0))

    def v_spec(i):
        return pl.BlockSpec((1, PAGE_SIZE, w),
                            lambda b, ph, g, pt: (pt[b, jnp.where(ph == 0, 0, g) * DEC_PAGES + i], 0, 0))

    grid_spec = pltpu.PrefetchScalarGridSpec(
        num_scalar_prefetch=1,
        grid=(s, 2, n_pages // DEC_PAGES),
        in_specs=[new, new, new] + [k_spec(i) for i in range(DEC_PAGES)] + [v_spec(i) for i in range(DEC_PAGES)],
        out_specs=new,
        scratch_shapes=[pltpu.VMEM((nb, w), F32), pltpu.VMEM((rows, nb), F32), pltpu.VMEM((rows, 1), F32),
                        pltpu.VMEM((rows, 1), F32), pltpu.VMEM((rows, w), F32)],
    )
    return pl.pallas_call(
        functools.partial(_moba_dec_kernel, t_new=t_new, past_len=past_len),
        grid_spec=grid_spec,
        out_shape=jax.ShapeDtypeStruct((s, DEC_TOK, w), F32),
        compiler_params=_cparams(("parallel", "arbitrary", "arbitrary")),
        name="moba_decode",
    )(page_table, qn, kn, vn, *([cache_k] * DEC_PAGES), *([cache_v] * DEC_PAGES))


IN0_SPLITS = ((0, A_COLS), (A_COLS, B_WIDTH), (A_COLS + B_WIDTH, B_WIDTH), (A_COLS + 2 * B_WIDTH, B_WIDTH))


def _pad_tokens(a, t_pad):
    return jnp.pad(a, ((0, 0), (0, t_pad - a.shape[1]), (0, 0)))


def kernel(x_prompt, x_sample, cache_l0_k, cache_l0_v, state_l0_wkv, state_l0_shift, state_l1_conv, page_table, norm0_mix_g, w_in0, mu_shift, w_lora_up, w0, a_lora_up, a0, g_lora_up, k_k, k_a, r_k, lnx_g, lnx_b, q_norm_g, k_norm_g, w_out0, norm0_ffn_g, ffn_w_gate, ffn_w_up, ffn_w_down, norm1_mix_g, conv_w_in, conv_dw_w, conv_dw_b, conv_ln_g, conv_ln_b, conv_w_out, norm1_ffn_g, moe_router, moe_w_gate, moe_w_up, moe_w_down):
    bp, tp, d = x_prompt.shape
    bs, ts, _ = x_sample.shape
    n_p, n_s = bp * tp, bs * DEC_TOK
    cast = lambda a: a.astype(BF16)
    w_in0_b, w_out0_b = cast(w_in0), cast(w_out0)
    ffn_g_b, ffn_u_b, ffn_d_b = cast(ffn_w_gate), cast(ffn_w_up), cast(ffn_w_down)
    conv_in_b, conv_out_b = cast(conv_w_in), cast(conv_w_out)
    moe_g_b, moe_u_b, moe_d_b = cast(moe_w_gate), cast(moe_w_up), cast(moe_w_down)
    rk = r_k.reshape(-1)
    rwkv_w = (mu_shift, w_lora_up, w0, a_lora_up, a0, g_lora_up, k_k, k_a, rk)

    xp = x_prompt.reshape(n_p, d)
    xs = _pad_tokens(x_sample, DEC_TOK).reshape(n_s, d)

    p_a, q, k, v_p = norm_proj(xp, norm0_mix_g, w_in0_b, IN0_SPLITS, 512)
    p_a3 = p_a.reshape(bp, tp, A_COLS)
    r_, lw_, kt_, vv_, kap_, b_, g_, bonus_ = rwkv_prep(
        p_a3, jnp.zeros((bp, A_COLS), F32), *rwkv_w, 256)
    y_p, wkv_prompt = rwkv_chunked(r_, lw_, kt_, vv_, kap_, b_,
                                   jnp.zeros((bp, A_HEADS, HEAD_DIM, HEAD_DIM), F32), RWKV_CHUNK, tp)
    qn, kn_p, km = moba_prep(q, k, q_norm_g, k_norm_g, MOBA_BLOCK)
    o_p = moba_prefill(qn.reshape(bp, tp, B_WIDTH), kn_p.reshape(bp, tp, B_WIDTH),
                       v_p.reshape(bp, tp, B_WIDTH), km.reshape(bp, tp // MOBA_BLOCK, B_WIDTH))
    flat = lambda a: a.reshape(-1, a.shape[-1])
    xp = mix_out(xp, flat(y_p), flat(bonus_), flat(g_), flat(o_p), lnx_g, lnx_b, w_out0_b, 512)
    xp = ffn(xp, norm0_ffn_g, ffn_g_b, ffn_u_b, ffn_d_b, 512, D_FF // 2)
    shift_prompt = p_a3[:, -1]

    ps_a, q, k, v_s = norm_proj(xs, norm0_mix_g, w_in0_b, IN0_SPLITS, n_s)
    ps_a3 = ps_a.reshape(bs, DEC_TOK, A_COLS)
    r_, lw_, kt_, vv_, kap_, b_, g_, bonus_ = rwkv_prep(ps_a3, state_l0_shift, *rwkv_w, DEC_TOK)
    y_s, wkv_sample = rwkv_chunked(r_, lw_, kt_, vv_, kap_, b_, state_l0_wkv, DEC_TOK, ts)
    qn, kn_s, _ = moba_prep(q, k, q_norm_g, k_norm_g, n_s)
    tok3 = lambda a: a.reshape(bs, DEC_TOK, B_WIDTH)
    pool = cache_l0_k.shape[0]
    o_s = moba_decode(tok3(qn), tok3(kn_s), tok3(v_s), cache_l0_k.reshape(pool, PAGE_SIZE, B_WIDTH),
                      cache_l0_v.reshape(pool, PAGE_SIZE, B_WIDTH), page_table, ts)
    xs = mix_out(xs, flat(y_s), flat(bonus_), flat(g_), flat(o_s), lnx_g, lnx_b, w_out0_b, n_s)
    xs = ffn(xs, norm0_ffn_g, ffn_g_b, ffn_u_b, ffn_d_b, n_s, D_FF // 2)
    shift_sample = ps_a3[:, ts - 1]

    u_p = conv_in(xp, norm1_mix_g, conv_in_b, 512).reshape(bp, tp, d)
    xp = conv_dw(u_p, jnp.zeros((bp, HALO, d), F32), xp.reshape(bp, tp, d), conv_dw_w, conv_dw_b,
                 conv_ln_g, conv_ln_b, conv_out_b, 256).reshape(n_p, d)
    xp = moe_layer(xp, norm1_ffn_g, moe_router, moe_g_b, moe_u_b, moe_d_b, MOE_ROWS)
    conv_prompt = u_p[:, tp - (CONV_K - 1):]

    u_s = conv_in(xs, norm1_mix_g, conv_in_b, n_s).reshape(bs, DEC_TOK, d)
    init = jnp.pad(state_l1_conv, ((0, 0), (HALO - (CONV_K - 1), 0), (0, 0)))
    xs = conv_dw(u_s, init, xs.reshape(bs, DEC_TOK, d), conv_dw_w, conv_dw_b,
                 conv_ln_g, conv_ln_b, conv_out_b, DEC_TOK).reshape(n_s, d)
    xs = moe_layer(xs, norm1_ffn_g, moe_router, moe_g_b, moe_u_b, moe_d_b, LANES)
    conv_sample = jnp.concatenate([state_l1_conv, u_s[:, :ts]], axis=1)[:, -(CONV_K - 1):]

    heads = lambda a, b_, t_: a.reshape(b_, -1, B_WIDTH)[:, :t_].reshape(b_, t_, B_HEADS, HEAD_DIM)
    return (xp.reshape(bp, tp, d), xs.reshape(bs, DEC_TOK, d)[:, :ts],
            heads(kn_p, bp, tp), heads(v_p, bp, tp), heads(kn_s, bs, ts), heads(v_s, bs, ts),
            wkv_prompt, wkv_sample, shift_prompt, shift_sample, conv_prompt, conv_sample)
```

```python
import functools

import jax
import jax.numpy as jnp
from jax import lax
from jax.experimental import pallas as pl
from jax.experimental.pallas import tpu as pltpu

F32 = jnp.float32
BF16 = jnp.bfloat16

D_MODEL = 1024
HEAD_DIM = 64
A_WIDTH = 512
A_HEADS = 8
B_WIDTH = 512
B_HEADS = 8
LORA_DECAY = 64
LORA_ICLR = 64
LORA_GATE = 128
A_COLS = 3 * A_WIDTH + LORA_DECAY + LORA_ICLR + LORA_GATE
MOBA_BLOCK = 256
MOBA_TOPK = 3
PAGE_SIZE = 128
ATTN_SCALE = HEAD_DIM ** -0.5
D_FF = 2816
N_EXPERTS = 8
D_FF_EXPERT = 3584
CONV_K = 31
RMS_EPS = 1e-6
GN_EPS = 64e-5
LN_EPS = 1e-5

LANES = 128
VMEM_LIMIT = 56 * 1024 * 1024
RWKV_CHUNK = 64
MOE_ROWS = 512
HALO = 32
HI = lax.Precision.HIGHEST


def _cparams(sem):
    return pltpu.CompilerParams(dimension_semantics=sem, vmem_limit_bytes=VMEM_LIMIT)


def _rms(x, g):
    return x * lax.rsqrt(jnp.mean(x * x, axis=-1, keepdims=True) + RMS_EPS) * g


def _bdot(a, b):
    return jnp.dot(a.astype(BF16), b.astype(BF16), preferred_element_type=F32)


def _dot_nt(a, b, precision=None):
    return lax.dot_general(a, b, (((1,), (1,)), ((), ())), precision=precision,
                           preferred_element_type=F32)


def _seg_sum(x, bd):
    hi = x.astype(BF16)
    lo = (x - hi.astype(F32)).astype(BF16)
    return (jnp.dot(hi, bd, preferred_element_type=F32)
            + jnp.dot(lo, bd, preferred_element_type=F32))


def _block_diag_ones(n, seg):
    i = jnp.arange(n)
    return (i[:, None] // seg == i[None, :] // seg).astype(BF16)


def _norm_proj_kernel(x_ref, g_ref, w_ref, *out_refs, splits):
    h = _rms(x_ref[...], g_ref[...]).astype(BF16)
    for o_ref, (c0, cn) in zip(out_refs, splits):
        o_ref[...] = jnp.dot(h, w_ref[:, c0:c0 + cn], preferred_element_type=F32)


def norm_proj(x, g, w_bf16, splits, tm):
    n, d = x.shape
    cols = w_bf16.shape[1]
    return pl.pallas_call(
        functools.partial(_norm_proj_kernel, splits=splits),
        grid=(n // tm,),
        in_specs=[pl.BlockSpec((tm, d), lambda i: (i, 0)),
                  pl.BlockSpec((1, d), lambda i: (0, 0)),
                  pl.BlockSpec((d, cols), lambda i: (0, 0))],
        out_specs=[pl.BlockSpec((tm, cn), lambda i: (i, 0)) for _, cn in splits],
        out_shape=[jax.ShapeDtypeStruct((n, cn), F32) for _, cn in splits],
        compiler_params=_cparams(("parallel",)),
        name="norm_proj",
    )(x, g.reshape(1, d), w_bf16)


def _rwkv_prep_kernel(p_ref, shift_ref, mu_ref, wl_ref, w0_ref, al_ref, a0_ref, gl_ref,
                      kk_ref, ka_ref, rk_ref, bd_ref,
                      r_o, lw_o, k_o, v_o, kap_o, b_o, g_o, bonus_o, carry):
    t = pl.program_id(1)

    @pl.when(t == 0)
    def _():
        carry[...] = shift_ref[0]

    p = p_ref[0]
    tt = p.shape[0]
    rolled = pltpu.roll(p, shift=1, axis=0)
    row = lax.broadcasted_iota(jnp.int32, p.shape, 0)
    prev = jnp.where(row == 0, carry[...], rolled)
    carry[...] = p[tt - 1:tt, :]
    xs = p + (prev - p) * mu_ref[...]
    aw = A_WIDTH
    r = xs[:, 0:aw]
    k = xs[:, aw:2 * aw]
    v = xs[:, 2 * aw:3 * aw]
    c0 = 3 * aw
    xw = xs[:, c0:c0 + LORA_DECAY]
    xa = xs[:, c0 + LORA_DECAY:c0 + LORA_DECAY + LORA_ICLR]
    xg = xs[:, c0 + LORA_DECAY + LORA_ICLR:]
    bd = bd_ref[...]

    z = w0_ref[...] + _bdot(jnp.tanh(xw), wl_ref[...])
    nz = -z
    softplus = jnp.maximum(nz, 0.0) + jnp.log(1.0 + jnp.exp(-jnp.abs(nz)))
    w_log = -softplus - 0.5
    lw_o[0] = -jnp.exp(w_log)
    a = jax.nn.sigmoid(a0_ref[...] + _bdot(xa, al_ref[...]))
    g_o[0] = _bdot(jax.nn.sigmoid(xg), gl_ref[...])
    kk = k * kk_ref[...]
    nrm = jnp.sqrt(_seg_sum(kk * kk, bd))
    kap = kk / jnp.maximum(nrm, 1e-12)
    k_h = k * (1.0 + (a - 1.0) * ka_ref[...])
    r_o[0] = r
    k_o[0] = k_h
    v_o[0] = v
    kap_o[0] = kap
    b_o[0] = kap * a
    bonus_o[0] = _seg_sum(r * k_h * rk_ref[...], bd) * v


def rwkv_prep(p_a, shift_prev, mu_shift, w_lora_up, w0, a_lora_up, a0, g_lora_up, k_k, k_a, r_k, tt):
    b, t, _ = p_a.shape
    row = lambda x: x.reshape(1, -1)
    const = lambda shape: pl.BlockSpec(shape, lambda i, j: (0,) * len(shape))
    tok = lambda w: pl.BlockSpec((1, tt, w), lambda i, j: (i, j, 0))
    outs = [jax.ShapeDtypeStruct((b, t, A_WIDTH), F32)] * 8
    return pl.pallas_call(
        _rwkv_prep_kernel,
        grid=(b, t // tt),
        in_specs=[tok(A_COLS),
                  pl.BlockSpec((1, 1, A_COLS), lambda i, j: (i, 0, 0)),
                  const((1, A_COLS)),
                  const((LORA_DECAY, A_WIDTH)), const((1, A_WIDTH)),
                  const((LORA_ICLR, A_WIDTH)), const((1, A_WIDTH)),
                  const((LORA_GATE, A_WIDTH)),
                  const((1, A_WIDTH)), const((1, A_WIDTH)), const((1, A_WIDTH)),
                  const((A_WIDTH, A_WIDTH))],
        out_specs=[tok(A_WIDTH)] * 8,
        out_shape=outs,
        scratch_shapes=[pltpu.VMEM((1, A_COLS), F32)],
        compiler_params=_cparams(("parallel", "arbitrary")),
        name="rwkv_prep",
    )(p_a, shift_prev.reshape(b, 1, A_COLS), row(mu_shift), w_lora_up.astype(BF16), row(w0),
      a_lora_up.astype(BF16), row(a0), g_lora_up.astype(BF16), row(k_k), row(k_a), row(r_k),
      _block_diag_ones(A_WIDTH, HEAD_DIM))


RWKV_PASSES_PREFILL = {"gram": 3, "inv": 1, "apply": 1, "state": 1}
RWKV_PASSES_DECODE = {"gram": 6, "inv": 6, "apply": 6, "state": 6}


def _split(x):
    hi = x.astype(BF16)
    return hi, (x - hi.astype(F32)).astype(BF16)


def _mm(a, b, passes, dims=(1, 0)):
    dn = (((dims[0],), (dims[1],)), ((), ()))
    dot = functools.partial(lax.dot_general, dimension_numbers=dn, preferred_element_type=F32)
    if passes == 6:
        return dot(a, b, precision=HI)
    if passes == 1:
        return dot(a.astype(BF16), b.astype(BF16))
    ah, al = _split(a)
    bh, bl = _split(b)
    return dot(ah, bh) + (dot(ah, bl) + dot(al, bh))


def _unit_lower_inverses(ls, n, c, passes):
    ri = lax.broadcasted_iota(jnp.int32, (n, n), 0)
    ci = lax.broadcasted_iota(jnp.int32, (n, n), 1)
    eye = (ri == ci).astype(F32)

    def same_block(bs):
        return (ri // bs) == (ci // bs)

    b0 = min(16, c)
    powers = [jnp.where(same_block(b0), -l, 0.0) for l in ls]
    invs = [eye + p for p in powers]
    span = 2
    while span < b0:
        powers = [_mm(p, p, passes) for p in powers]
        invs = [i + _mm(i, p, passes) for i, p in zip(invs, powers)]
        span *= 2
    bs = b0
    while bs < c:
        cross = same_block(2 * bs) & jnp.logical_not(same_block(bs))
        mids = [_mm(i, jnp.where(cross, l, 0.0), passes) for i, l in zip(invs, ls)]
        invs = [i - _mm(m, i, passes) for i, m in zip(invs, mids)]
        bs *= 2
    return invs


def _rwkv_chunk_kernel(r_ref, lw_ref, k_ref, v_ref, kap_ref, b_ref, s0_ref, y_ref, sout_ref, state,
                       *, chunk, t_valid, pw):
    ci = pl.program_id(1)
    c = chunk
    n = 2 * c
    hd = HEAD_DIM
    pairs = [(bi, j) for bi in range(r_ref.shape[0]) for j in range(A_HEADS // 2)]

    @pl.when(ci == 0)
    def _():
        z = jnp.zeros((hd, hd), F32)
        for idx, (bi, j) in enumerate(pairs):
            state[idx] = jnp.concatenate(
                [jnp.concatenate([s0_ref[bi, 2 * j], z], axis=1),
                 jnp.concatenate([z, s0_ref[bi, 2 * j + 1]], axis=1)], axis=0)

    lane = lax.broadcasted_iota(jnp.int32, (c, LANES), 1)
    trow = lax.broadcasted_iota(jnp.int32, (c, LANES), 0) + ci * c
    live = trow < t_valid
    m0 = lane < hd
    tri = (lax.broadcasted_iota(jnp.int32, (c, c), 0)
           >= lax.broadcasted_iota(jnp.int32, (c, c), 1)).astype(BF16)
    ri = lax.broadcasted_iota(jnp.int32, (n, n), 0)
    cj = lax.broadcasted_iota(jnp.int32, (n, n), 1)
    same = (ri // c) == (cj // c)
    strict = same & (ri > cj)
    incl = same & (ri >= cj)

    def stack(x):
        return jnp.concatenate([jnp.where(m0, x, 0.0), jnp.where(m0, 0.0, x)], axis=0)

    def cumulative(x):
        hi, lo = _split(x)
        lo2 = (x - hi.astype(F32) - lo.astype(F32)).astype(BF16)
        dot = functools.partial(jnp.dot, preferred_element_type=F32)
        return dot(tri, hi) + (dot(tri, lo) + dot(tri, lo2))

    each = lambda f, *cols: [f(*xs) for xs in zip(*cols)]
    sls = [(bi, slice(j * LANES, (j + 1) * LANES)) for bi, j in pairs]
    ld = lambda ref: [jnp.where(live, ref[bi, :, sl], 0.0) for bi, sl in sls]
    r = [r_ref[bi, :, sl] for bi, sl in sls]
    lw, k, v, kap, b = ld(lw_ref), ld(k_ref), ld(v_ref), ld(kap_ref), ld(b_ref)
    cum = each(cumulative, lw)
    total = [x[c - 1:c, :] for x in cum]
    ginv = [jnp.exp(-x) for x in cum]
    tail = each(lambda t, x: jnp.exp(t - x), total, cum)
    kap_s = each(lambda a, x, w: stack(a * jnp.exp(x - w)), kap, cum, lw)
    r_s = each(lambda a, x: stack(a * jnp.exp(x)), r, cum)
    b_s = each(lambda a, g: stack(a * g), b, ginv)
    k_s = each(lambda a, g: stack(a * g), k, ginv)
    v_s = each(stack, v)
    kbar_s = each(lambda a, t: stack(a * t), k, tail)
    bbar_s = each(lambda a, t: stack(a * t), b, tail)
    gram = lambda x, y: _mm(x, y, pw["gram"], (1, 1))
    l_b = each(lambda x, y: jnp.where(strict, gram(x, y), 0.0), kap_s, b_s)
    l_k = each(lambda x, y: jnp.where(strict, gram(x, y), 0.0), kap_s, k_s)
    a_qk = each(lambda x, y: jnp.where(incl, gram(x, y), 0.0), r_s, k_s)
    a_qb = each(lambda x, y: jnp.where(incl, gram(x, y), 0.0), r_s, b_s)
    inv = _unit_lower_inverses(l_b, n, c, pw["inv"])
    app = lambda x, y: _mm(x, y, pw["apply"])
    w_s = each(app, inv, kap_s)
    u0_s = each(app, inv, each(app, l_k, v_s))
    y0_s = each(app, a_qk, v_s)

    st = lambda x, y, dims=(1, 0): _mm(x, y, pw["state"], dims)
    s = [state[idx] for idx in range(len(pairs))]
    u_s = each(lambda w, s_, u0: st(w, s_, (1, 1)) + u0, w_s, s, u0_s)
    y_s = each(lambda r_, s_, y0, a, u: st(r_, s_, (1, 1)) + y0 - st(a, u), r_s, s, y0_s, a_qb, u_s)
    s_new = each(lambda s_, t, v_, kb, u, bb: s_ * jnp.exp(t) + st(v_, kb, (0, 0)) - st(u, bb, (0, 0)),
                 s, total, v_s, kbar_s, u_s, bbar_s)
    for idx, (bi, sl) in enumerate(sls):
        y_ref[bi, :, sl] = y_s[idx][0:c] + y_s[idx][c:n]
        state[idx] = s_new[idx]

    @pl.when(ci == pl.num_programs(1) - 1)
    def _():
        for idx, (bi, j) in enumerate(pairs):
            s = state[idx]
            sout_ref[bi, 2 * j] = s[0:hd, 0:hd]
            sout_ref[bi, 2 * j + 1] = s[hd:2 * hd, hd:2 * hd]


def rwkv_chunked(r, lw, k, v, kap, b, s0, chunk, t_valid, bb, passes):
    bsz, t, w = r.shape
    tok = pl.BlockSpec((bb, chunk, w), lambda i, c: (i, c, 0))
    st = pl.BlockSpec((bb, A_HEADS, HEAD_DIM, HEAD_DIM), lambda i, c: (i, 0, 0, 0))
    y, s_out = pl.pallas_call(
        functools.partial(_rwkv_chunk_kernel, chunk=chunk, t_valid=t_valid, pw=passes),
        grid=(bsz // bb, t // chunk),
        in_specs=[tok] * 6 + [st],
        out_specs=[tok, st],
        out_shape=[jax.ShapeDtypeStruct((bsz, t, w), F32),
                   jax.ShapeDtypeStruct((bsz, A_HEADS, HEAD_DIM, HEAD_DIM), F32)],
        scratch_shapes=[pltpu.VMEM((bb * A_HEADS // 2, LANES, LANES), F32)],
        compiler_params=_cparams(("parallel", "arbitrary")),
        name="rwkv_chunk",
    )(r, lw, k, v, kap, b, s0)
    return y, s_out


def _mix_out_kernel(x_ref, y_ref, bonus_ref, g_ref, o_ref, lng_ref, lnb_ref, bd_ref, w_ref, out_ref):
    y = y_ref[...]
    bd = bd_ref[...]
    mu = _seg_sum(y, bd) * (1.0 / HEAD_DIM)
    d = y - mu
    var = _seg_sum(d * d, bd) * (1.0 / HEAD_DIM)
    yn = d * lax.rsqrt(var + GN_EPS) * lng_ref[...] + lnb_ref[...]
    ya = (yn + bonus_ref[...]) * g_ref[...]
    acc = jnp.dot(ya.astype(BF16), w_ref[0:A_WIDTH, :], preferred_element_type=F32)
    acc += jnp.dot(o_ref[...].astype(BF16), w_ref[A_WIDTH:, :], preferred_element_type=F32)
    out_ref[...] = x_ref[...] + acc


def mix_out(x, y, bonus, g, o, lnx_g, lnx_b, w_out_bf16, tm):
    n, d = x.shape
    tok = lambda w: pl.BlockSpec((tm, w), lambda i: (i, 0))
    const = lambda shape: pl.BlockSpec(shape, lambda i: (0,) * len(shape))
    return pl.pallas_call(
        _mix_out_kernel,
        grid=(n // tm,),
        in_specs=[tok(d), tok(A_WIDTH), tok(A_WIDTH), tok(A_WIDTH), tok(B_WIDTH),
                  const((1, A_WIDTH)), const((1, A_WIDTH)), const((A_WIDTH, A_WIDTH)),
                  const((A_WIDTH + B_WIDTH, d))],
        out_specs=tok(d),
        out_shape=jax.ShapeDtypeStruct((n, d), F32),
        compiler_params=_cparams(("parallel",)),
        name="mix_out",
    )(x, y, bonus, g, o, lnx_g.reshape(1, -1), lnx_b.reshape(1, -1),
      _block_diag_ones(A_WIDTH, HEAD_DIM), w_out_bf16)


def _ffn_kernel(x_ref, g_ref, wg_ref, wu_ref, wd_ref, out_ref, h_scr, acc):
    j = pl.program_id(1)

    @pl.when(j == 0)
    def _():
        h_scr[...] = _rms(x_ref[...], g_ref[...]).astype(BF16)
        acc[...] = jnp.zeros_like(acc)

    h = h_scr[...]
    gate = jnp.dot(h, wg_ref[...], preferred_element_type=F32)
    up = jnp.dot(h, wu_ref[...], preferred_element_type=F32)
    act = (jax.nn.silu(gate) * up).astype(BF16)
    acc[...] += jnp.dot(act, wd_ref[...], preferred_element_type=F32)

    @pl.when(j == pl.num_programs(1) - 1)
    def _():
        out_ref[...] = x_ref[...] + acc[...]


def ffn(x, g, wg, wu, wd, tm, tf):
    n, d = x.shape
    f = wg.shape[1]
    return pl.pallas_call(
        _ffn_kernel,
        grid=(n // tm, f // tf),
        in_specs=[pl.BlockSpec((tm, d), lambda i, j: (i, 0)),
                  pl.BlockSpec((1, d), lambda i, j: (0, 0)),
                  pl.BlockSpec((d, tf), lambda i, j: (0, j)),
                  pl.BlockSpec((d, tf), lambda i, j: (0, j)),
                  pl.BlockSpec((tf, d), lambda i, j: (j, 0))],
        out_specs=pl.BlockSpec((tm, d), lambda i, j: (i, 0)),
        out_shape=jax.ShapeDtypeStruct((n, d), F32),
        scratch_shapes=[pltpu.VMEM((tm, d), BF16), pltpu.VMEM((tm, d), F32)],
        compiler_params=_cparams(("parallel", "arbitrary")),
        name="ffn",
    )(x, g.reshape(1, d), wg, wu, wd)


def _conv_in_kernel(x_ref, g_ref, w_ref, u_ref):
    h = _rms(x_ref[...], g_ref[...]).astype(BF16)
    d = u_ref.shape[1]
    a = jnp.dot(h, w_ref[:, 0:d], preferred_element_type=F32)
    gate = jnp.dot(h, w_ref[:, d:2 * d], preferred_element_type=F32)
    u_ref[...] = a * jax.nn.sigmoid(gate)


def conv_in(x, g, w_bf16, tm):
    n, d = x.shape
    return pl.pallas_call(
        _conv_in_kernel,
        grid=(n // tm,),
        in_specs=[pl.BlockSpec((tm, d), lambda i: (i, 0)),
                  pl.BlockSpec((1, d), lambda i: (0, 0)),
                  pl.BlockSpec((d, 2 * d), lambda i: (0, 0))],
        out_specs=pl.BlockSpec((tm, d), lambda i: (i, 0)),
        out_shape=jax.ShapeDtypeStruct((n, d), F32),
        compiler_params=_cparams(("parallel",)),
        name="conv_in",
    )(x, g.reshape(1, d), w_bf16)


def _conv_dw_kernel(u_ref, init_ref, x_ref, dw_ref, db_ref, lg_ref, lb_ref, w_ref, out_ref, ctx):
    t = pl.program_id(1)
    tt = u_ref.shape[1]

    @pl.when(t == 0)
    def _():
        ctx[0:HALO, :] = init_ref[0]

    ctx[HALO:HALO + tt, :] = u_ref[0]
    acc = jnp.zeros((tt, D_MODEL), F32) + db_ref[...]
    for j in range(CONV_K):
        acc = acc + ctx[pl.ds(HALO - (CONV_K - 1) + j, tt), :] * dw_ref[j:j + 1, :]
    keep = ctx[tt:tt + HALO, :]
    ctx[0:HALO, :] = keep
    mu = jnp.mean(acc, axis=-1, keepdims=True)
    d = acc - mu
    var = jnp.mean(d * d, axis=-1, keepdims=True)
    cf = d * lax.rsqrt(var + LN_EPS) * lg_ref[...] + lb_ref[...]
    out_ref[0] = x_ref[0] + jnp.dot(jax.nn.silu(cf).astype(BF16), w_ref[...], preferred_element_type=F32)


def conv_dw(u, init, x, dw_w, dw_b, ln_g, ln_b, w_out_bf16, tt):
    b, t, d = u.shape
    dw = jnp.pad(dw_w, ((0, HALO - CONV_K), (0, 0)))
    tok = pl.BlockSpec((1, tt, d), lambda i, j: (i, j, 0))
    const = lambda shape: pl.BlockSpec(shape, lambda i, j: (0,) * len(shape))
    return pl.pallas_call(
        _conv_dw_kernel,
        grid=(b, t // tt),
        in_specs=[tok, pl.BlockSpec((1, HALO, d), lambda i, j: (i, 0, 0)), tok,
                  const((HALO, d)), const((1, d)), const((1, d)), const((1, d)), const((d, d))],
        out_specs=tok,
        out_shape=jax.ShapeDtypeStruct((b, t, d), F32),
        scratch_shapes=[pltpu.VMEM((HALO + tt, d), F32)],
        compiler_params=_cparams(("parallel", "arbitrary")),
        name="conv_dw",
    )(u, init, x, dw, dw_b.reshape(1, d), ln_g.reshape(1, d), ln_b.reshape(1, d), w_out_bf16)


def _router_kernel(x_ref, g_ref, wr_ref, idx_ref, pw_ref):
    h = _rms(x_ref[...], g_ref[...])
    logits = _bdot(h, wr_ref[...])
    lane = lax.broadcasted_iota(jnp.int32, logits.shape, 1)
    valid = lane < N_EXPERTS
    logits = jnp.where(valid, logits, -jnp.inf)
    e = jnp.exp(logits - jnp.max(logits, axis=-1, keepdims=True))
    probs = jnp.where(valid, e / jnp.sum(e, axis=-1, keepdims=True), -1.0)
    m1 = jnp.max(probs, axis=-1, keepdims=True)
    i1 = jnp.min(jnp.where(probs == m1, lane, LANES), axis=-1, keepdims=True)
    rest = jnp.where(lane == i1, -1.0, probs)
    m2 = jnp.max(rest, axis=-1, keepdims=True)
    i2 = jnp.min(jnp.where(rest == m2, lane, LANES), axis=-1, keepdims=True)
    den = m1 + m2
    idx_ref[...] = jnp.where(lane == 0, i1, jnp.where(lane == 1, i2, 0))
    pw_ref[...] = jnp.where(lane == 0, m1 / den, jnp.where(lane == 1, m2 / den, 0.0))


def router(x, g, w_router, tm):
    n, d = x.shape
    wr = jnp.pad(w_router, ((0, 0), (0, LANES - N_EXPERTS))).astype(BF16)
    return pl.pallas_call(
        _router_kernel,
        grid=(n // tm,),
        in_specs=[pl.BlockSpec((tm, d), lambda i: (i, 0)),
                  pl.BlockSpec((1, d), lambda i: (0, 0)),
                  pl.BlockSpec((d, LANES), lambda i: (0, 0))],
        out_specs=[pl.BlockSpec((tm, LANES), lambda i: (i, 0))] * 2,
        out_shape=[jax.ShapeDtypeStruct((n, LANES), jnp.int32), jax.ShapeDtypeStruct((n, LANES), F32)],
        compiler_params=_cparams(("parallel",)),
        name="moe_router",
    )(x, g.reshape(1, d), wr)


def _moe_kernel(be_ref, nu_ref, tok_ref, dst_ref,
                x_hbm, g_ref, p_ref, wg_ref, wu_ref, wd_ref,
                y_hbm, xg, h_scr, acc, sem_in, sem_out, *, n_dst):
    i = pl.program_id(0)
    j = pl.program_id(1)
    m = xg.shape[0]
    valid = i < nu_ref[0]
    base = i * m

    def row_in(r, tok):
        return pltpu.make_async_copy(x_hbm.at[pl.ds(tok, 1), :], xg.at[pl.ds(r, 1), :], sem_in)

    def row_out(r, dst):
        return pltpu.make_async_copy(xg.at[pl.ds(r, 1), :], y_hbm.at[pl.ds(dst, 1), :], sem_out)

    @pl.when(valid & (j == 0))
    def _():
        def start(r, c):
            row_in(r, tok_ref[base + r]).start()
            return c

        def wait(r, c):
            row_in(r, 0).wait()
            return c

        lax.fori_loop(0, m, start, 0)
        lax.fori_loop(0, m, wait, 0)
        h_scr[...] = _rms(xg[...], g_ref[...]).astype(BF16)
        acc[...] = jnp.zeros_like(acc)

    @pl.when(valid)
    def _():
        h = h_scr[...]
        gate = jnp.dot(h, wg_ref[0], preferred_element_type=F32)
        up = jnp.dot(h, wu_ref[0], preferred_element_type=F32)
        act = (jax.nn.silu(gate) * up).astype(BF16)
        acc[...] += jnp.dot(act, wd_ref[0], preferred_element_type=F32)

    @pl.when(valid & (j == pl.num_programs(1) - 1))
    def _():
        xg[...] = acc[...] * p_ref[...]

        def start(r, c):
            dst = dst_ref[base + r]

            @pl.when(dst < n_dst)
            def _():
                row_out(r, dst).start()
            return c

        def wait(r, c):
            dst = dst_ref[base + r]

            @pl.when(dst < n_dst)
            def _():
                row_out(r, 0).wait()
            return c

        lax.fori_loop(0, m, start, 0)
        lax.fori_loop(0, m, wait, 0)


def moe_experts(x, g, block_e, n_used, buf_tok, buf_dst, buf_p, wg, wu, wd, m, tf):
    n, d = x.shape
    n_blocks = block_e.shape[0]
    nj = D_FF_EXPERT // tf

    def jj(i, j, nu):
        return jnp.where(i < nu[0], j, nj - 1)

    grid_spec = pltpu.PrefetchScalarGridSpec(
        num_scalar_prefetch=4,
        grid=(n_blocks, nj),
        in_specs=[pl.BlockSpec(memory_space=pl.ANY),
                  pl.BlockSpec((1, d), lambda i, j, be, nu, tk, ds: (0, 0)),
                  pl.BlockSpec((m, 1), lambda i, j, be, nu, tk, ds: (i, 0)),
                  pl.BlockSpec((1, d, tf), lambda i, j, be, nu, tk, ds: (be[i], 0, jj(i, j, nu))),
                  pl.BlockSpec((1, d, tf), lambda i, j, be, nu, tk, ds: (be[i], 0, jj(i, j, nu))),
                  pl.BlockSpec((1, tf, d), lambda i, j, be, nu, tk, ds: (be[i], jj(i, j, nu), 0))],
        out_specs=pl.BlockSpec(memory_space=pl.ANY),
        scratch_shapes=[pltpu.VMEM((m, d), F32), pltpu.VMEM((m, d), BF16), pltpu.VMEM((m, d), F32),
                        pltpu.SemaphoreType.DMA, pltpu.SemaphoreType.DMA],
    )
    return pl.pallas_call(
        functools.partial(_moe_kernel, n_dst=2 * n),
        grid_spec=grid_spec,
        out_shape=jax.ShapeDtypeStruct((2 * n, d), F32),
        compiler_params=_cparams(("arbitrary", "arbitrary")),
        name="moe_experts",
    )(block_e, n_used, buf_tok, buf_dst, x, g.reshape(1, d), buf_p, wg, wu, wd)


def _combine_kernel(x_ref, a_ref, b_ref, o_ref):
    o_ref[...] = x_ref[...] + (a_ref[0] + b_ref[0])


def moe_combine(x, y2, tm):
    n, d = x.shape
    y3 = y2.reshape(2, n, d)
    return pl.pallas_call(
        _combine_kernel,
        grid=(n // tm,),
        in_specs=[pl.BlockSpec((tm, d), lambda i: (i, 0)),
                  pl.BlockSpec((1, tm, d), lambda i: (0, i, 0)),
                  pl.BlockSpec((1, tm, d), lambda i: (1, i, 0))],
        out_specs=pl.BlockSpec((tm, d), lambda i: (i, 0)),
        out_shape=jax.ShapeDtypeStruct((n, d), F32),
        compiler_params=_cparams(("parallel",)),
        name="moe_combine",
    )(x, y3, y3)


def moe_layer(x, g, w_router, wg, wu, wd, m):
    n, d = x.shape
    idx, pw = router(x, g, w_router, min(n, 512))
    flat_e = idx[:, :2].reshape(-1)
    flat_p = pw[:, :2].reshape(-1)
    onehot = (flat_e[:, None] == jnp.arange(N_EXPERTS, dtype=jnp.int32)[None, :]).astype(jnp.int32)
    rank = jnp.sum((jnp.cumsum(onehot, axis=0) - onehot) * onehot, axis=1)
    counts = jnp.sum(onehot, axis=0)
    padded = (counts + m - 1) // m * m
    pend = jnp.cumsum(padded)
    slot = (pend - padded)[flat_e] + rank
    n_blocks = (2 * n) // m + N_EXPERTS
    n_slots = n_blocks * m
    pair = jnp.arange(2 * n, dtype=jnp.int32)
    buf_tok = jnp.zeros((n_slots,), jnp.int32).at[slot].set(pair // 2)
    buf_dst = jnp.full((n_slots,), 2 * n, jnp.int32).at[slot].set((pair % 2) * n + pair // 2)
    buf_p = jnp.zeros((n_slots,), F32).at[slot].set(flat_p).reshape(n_slots, 1)
    block_e = jnp.minimum(jnp.searchsorted(pend, jnp.arange(n_blocks, dtype=jnp.int32) * m, side='right'),
                          N_EXPERTS - 1).astype(jnp.int32)
    n_used = (pend[-1] // m).astype(jnp.int32).reshape(1)
    y2 = moe_experts(x, g, block_e, n_used, buf_tok, buf_dst, buf_p, wg, wu, wd, m, 512)
    return moe_combine(x, y2, min(n, 512))


def _moba_prep_kernel(q_ref, k_ref, qg_ref, kg_ref, bd_ref, qn_ref, kn_ref, km_ref):
    bd = bd_ref[...]

    def head_norm(x, g):
        ms = _seg_sum(x * x, bd) * (1.0 / HEAD_DIM)
        return x * lax.rsqrt(ms + RMS_EPS) * g

    qn_ref[...] = head_norm(q_ref[...], qg_ref[...])
    kn = head_norm(k_ref[...], kg_ref[...])
    kn_ref[...] = kn
    km_ref[0] = jnp.mean(kn, axis=0, keepdims=True)


def moba_prep(q, k, q_norm_g, k_norm_g, tm):
    n, w = q.shape
    tok = pl.BlockSpec((tm, w), lambda i: (i, 0))
    const = lambda shape: pl.BlockSpec(shape, lambda i: (0,) * len(shape))
    return pl.pallas_call(
        _moba_prep_kernel,
        grid=(n // tm,),
        in_specs=[tok, tok, const((1, w)), const((1, w)), const((w, w))],
        out_specs=[tok, tok, pl.BlockSpec((1, 1, w), lambda i: (i, 0, 0))],
        out_shape=[jax.ShapeDtypeStruct((n, w), F32), jax.ShapeDtypeStruct((n, w), F32),
                   jax.ShapeDtypeStruct((n // tm, 1, w), F32)],
        compiler_params=_cparams(("parallel",)),
        name="moba_prep",
    )(q, k, jnp.tile(q_norm_g, B_HEADS).reshape(1, w), jnp.tile(k_norm_g, B_HEADS).reshape(1, w),
      _block_diag_ones(w, HEAD_DIM))


def _top_blocks(gate, bidx, nb):
    chosen = jnp.zeros(gate.shape, F32)
    for _ in range(MOBA_TOPK):
        mx = jnp.max(gate, axis=-1, keepdims=True)
        first = jnp.min(jnp.where(gate == mx, bidx, nb), axis=-1, keepdims=True)
        chosen = jnp.where((bidx == first) & (mx > -jnp.inf), 1.0, chosen)
        gate = jnp.where(bidx == first, -jnp.inf, gate)
    return chosen


def _moba_prep_t_kernel(q_ref, k_ref, v_ref, qg_ref, kg_ref, bd_ref, kn_ref, km_ref, qt_ref, kb_ref, vt_ref):
    bd = bd_ref[...]

    def head_norm(x, g):
        ms = _seg_sum(x * x, bd) * (1.0 / HEAD_DIM)
        return x * lax.rsqrt(ms + RMS_EPS) * g

    qn = head_norm(q_ref[...], qg_ref[...])
    kn = head_norm(k_ref[...], kg_ref[...])
    kn_ref[...] = kn
    kb_ref[...] = kn.astype(BF16)
    km_ref[0] = jnp.mean(kn, axis=0, keepdims=True)
    qt_ref[0] = (qn * ATTN_SCALE).T.astype(BF16)
    vt_ref[0] = v_ref[...].T.astype(BF16)


def moba_prep_t(q, k, v, q_norm_g, k_norm_g, bsz):
    n, w = q.shape
    tm = MOBA_BLOCK
    nblk = n // bsz // tm
    tok = pl.BlockSpec((tm, w), lambda i: (i, 0))
    feat = pl.BlockSpec((1, w, tm), lambda i: (i // nblk, 0, i % nblk))
    const = lambda shape: pl.BlockSpec(shape, lambda i: (0,) * len(shape))
    return pl.pallas_call(
        _moba_prep_t_kernel,
        grid=(n // tm,),
        in_specs=[tok, tok, tok, const((1, w)), const((1, w)), const((w, w))],
        out_specs=[tok, pl.BlockSpec((1, 1, w), lambda i: (i, 0, 0)), feat, tok, feat],
        out_shape=[jax.ShapeDtypeStruct((n, w), F32), jax.ShapeDtypeStruct((n // tm, 1, w), F32),
                   jax.ShapeDtypeStruct((bsz, w, n // bsz), BF16), jax.ShapeDtypeStruct((n, w), BF16),
                   jax.ShapeDtypeStruct((bsz, w, n // bsz), BF16)],
        compiler_params=_cparams(("parallel",)),
        name="moba_prep_t",
    )(q, k, v, jnp.tile(q_norm_g, B_HEADS).reshape(1, w), jnp.tile(k_norm_g, B_HEADS).reshape(1, w),
      _block_diag_ones(w, HEAD_DIM))


def _moba_attn_kernel(qi_ref, ki_ref, qt_ref, k_ref, vt_ref, km_ref, o_ref,
                      qz, sel, bias, s_scr, m_scr, l_scr, acc):
    s = pl.program_id(1)
    qi = qi_ref[s]
    ki = ki_ref[s]
    blk = MOBA_BLOCK
    nb = km_ref.shape[1]
    hd = HEAD_DIM
    is_own = ki == qi
    key_i = lax.broadcasted_iota(jnp.int32, (blk, blk), 0)
    qry_i = lax.broadcasted_iota(jnp.int32, (blk, blk), 1)
    rel = key_i - qry_i
    blk_i = lax.broadcasted_iota(jnp.int32, (nb, blk), 0)
    step_off = jnp.full((1, 1), (qi - ki) * blk, jnp.int32).astype(F32)

    slopes = [2.0 ** (-8.0 * (h + 1) / B_HEADS) for h in range(B_HEADS)]

    def scores(h, own):
        hp = h // 2
        raw = jnp.dot(k_ref[0, :, hp * LANES:(hp + 1) * LANES], qz[h], preferred_element_type=F32)
        sp = raw + bias[h]
        if own:
            sp = jnp.where(rel <= 0, sp, -jnp.inf)
        s_scr[h] = sp
        return jnp.max(sp, axis=0, keepdims=True)

    def attend(h, own, top):
        if own:
            m_new = top
            p = jnp.exp(s_scr[h] - m_new)
            l_scr[h] = jnp.sum(p, axis=0, keepdims=True)
        else:
            live = sel[h, pl.ds(ki, 1), :] > 0.5
            off = slopes[h] * step_off
            m_prev = m_scr[h]
            m_new = jnp.maximum(m_prev, jnp.where(live, top - off, -jnp.inf))
            alpha = jnp.exp(m_prev - m_new)
            p = jnp.exp(s_scr[h] - jnp.where(live, m_new + off, jnp.inf))
            l_scr[h] = alpha * l_scr[h] + jnp.sum(p, axis=0, keepdims=True)
        pv = jnp.dot(vt_ref[0, h * hd:(h + 1) * hd, :], p.astype(BF16), preferred_element_type=F32)
        acc[h] = pv if own else alpha * acc[h] + pv
        m_scr[h] = m_new

    def all_heads(own):
        tops = [scores(h, own) for h in range(B_HEADS)]
        for h in range(B_HEADS):
            attend(h, own, tops[h])

    @pl.when(is_own)
    def _():
        frow = lax.broadcasted_iota(jnp.int32, (LANES, blk), 0)
        for h in range(B_HEADS):
            hp = h // 2
            qt = qt_ref[0, hp * LANES:(hp + 1) * LANES, :]
            mine = (frow < hd) if h % 2 == 0 else (frow >= hd)
            qh = jnp.where(mine, qt, jnp.zeros_like(qt))
            qz[h] = qh
            gate = jnp.dot(km_ref[0, :, hp * LANES:(hp + 1) * LANES].astype(BF16), qh,
                           preferred_element_type=F32)
            gate = jnp.where(blk_i < qi, gate, -jnp.inf)
            chosen = jnp.zeros((nb, blk), F32)
            for _ in range(MOBA_TOPK):
                mx = jnp.max(gate, axis=0, keepdims=True)
                first = jnp.min(jnp.where(gate == mx, blk_i, nb), axis=0, keepdims=True)
                chosen = jnp.where((blk_i == first) & (mx > -jnp.inf), 1.0, chosen)
                gate = jnp.where(blk_i == first, -jnp.inf, gate)
            sel[h] = chosen
            bias[h] = rel.astype(F32) * slopes[h]
        all_heads(True)

    @pl.when(jnp.logical_not(is_own))
    def _():
        all_heads(False)

    @pl.when((ki == qi - 1) | (qi == 0))
    def _():
        for hp in range(B_HEADS // 2):
            pair = jnp.concatenate([acc[2 * hp] / l_scr[2 * hp], acc[2 * hp + 1] / l_scr[2 * hp + 1]], axis=0)
            o_ref[0, :, hp * LANES:(hp + 1) * LANES] = pair.T


def moba_prefill(qt, kb, vt, km):
    b, w, t = qt.shape
    blk = MOBA_BLOCK
    nb = t // blk
    pairs = [(qi, ki) for qi in range(nb) for ki in [qi] + list(range(qi))]
    qi_tab = jnp.array([p[0] for p in pairs], jnp.int32)
    ki_tab = jnp.array([p[1] for p in pairs], jnp.int32)
    grid_spec = pltpu.PrefetchScalarGridSpec(
        num_scalar_prefetch=2,
        grid=(b, len(pairs)),
        in_specs=[pl.BlockSpec((1, w, blk), lambda i, s, qt_, kt_: (i, 0, qt_[s])),
                  pl.BlockSpec((1, blk, w), lambda i, s, qt_, kt_: (i, kt_[s], 0)),
                  pl.BlockSpec((1, w, blk), lambda i, s, qt_, kt_: (i, 0, kt_[s])),
                  pl.BlockSpec((1, nb, w), lambda i, s, qt_, kt_: (i, 0, 0))],
        out_specs=pl.BlockSpec((1, blk, w), lambda i, s, qt_, kt_: (i, qt_[s], 0)),
        scratch_shapes=[pltpu.VMEM((B_HEADS, LANES, blk), BF16), pltpu.VMEM((B_HEADS, nb, blk), F32),
                        pltpu.VMEM((B_HEADS, blk, blk), F32), pltpu.VMEM((B_HEADS, blk, blk), F32),
                        pltpu.VMEM((B_HEADS, 1, blk), F32),
                        pltpu.VMEM((B_HEADS, 1, blk), F32), pltpu.VMEM((B_HEADS, HEAD_DIM, blk), F32)],
    )
    return pl.pallas_call(
        _moba_attn_kernel,
        grid_spec=grid_spec,
        out_shape=jax.ShapeDtypeStruct((b, t, w), F32),
        compiler_params=_cparams(("parallel", "arbitrary")),
        name="moba_prefill",
    )(qi_tab, ki_tab, qt, kb, vt, km)


DEC_TOK = 8
DEC_PAGES = 8


def _moba_dec_kernel(pt_ref, q_ref, kn_ref, vn_ref, *refs, t_new, past_len):
    npg = DEC_PAGES
    k_refs, v_refs = refs[:npg], refs[npg:2 * npg]
    o_ref, km_scr, sel, m_scr, l_scr, acc = refs[2 * npg:]
    ph = pl.program_id(1)
    g = pl.program_id(2)
    last_g = g == pl.num_programs(2) - 1
    rows = B_HEADS * DEC_TOK
    w = B_WIDTH
    nb = km_scr.shape[0]
    per_blk = MOBA_BLOCK // PAGE_SIZE

    qt = jnp.concatenate([q_ref[0]] * B_HEADS, axis=0)
    head_of_lane = lax.broadcasted_iota(jnp.int32, (rows, w), 1) // HEAD_DIM
    head_of_row = lax.broadcasted_iota(jnp.int32, (rows, w), 0) // DEC_TOK
    own_lanes = head_of_lane == head_of_row
    qs = jnp.where(own_lanes, qt, 0.0).astype(BF16)
    rcol = lax.broadcasted_iota(jnp.int32, (rows, 1), 0)
    tok = rcol % DEC_TOK
    slope = jnp.exp2(-(rcol // DEC_TOK + 1).astype(F32))
    bidx = lax.broadcasted_iota(jnp.int32, (rows, nb), 1)

    def update(sc, vals):
        m_prev = m_scr[...]
        m_new = jnp.maximum(m_prev, jnp.max(sc, axis=-1, keepdims=True))
        alpha = jnp.exp(m_prev - m_new)
        p = jnp.exp(sc - m_new)
        l_scr[...] = alpha * l_scr[...] + jnp.sum(p, axis=-1, keepdims=True)
        acc[...] = alpha * acc[...] + jnp.dot(p.astype(BF16), vals, preferred_element_type=F32)
        m_scr[...] = m_new

    @pl.when(ph == 0)
    def _():
        for i in range(0, npg, per_blk):
            tot = jnp.sum(k_refs[i][0], axis=0, keepdims=True)
            for u in range(1, per_blk):
                tot = tot + jnp.sum(k_refs[i + u][0], axis=0, keepdims=True)
            km_scr[pl.ds(g * (npg // per_blk) + i // per_blk, 1), :] = tot * (1.0 / MOBA_BLOCK)

        @pl.when(last_g)
        def _():
            gate = _dot_nt(qs, km_scr[...].astype(BF16))
            sel[...] = _top_blocks(gate, bidx, nb)
            c = lax.broadcasted_iota(jnp.int32, (rows, DEC_TOK), 1)
            sc = (_dot_nt(qs, kn_ref[0].astype(BF16)) * ATTN_SCALE
                  - slope * (tok - c).astype(F32))
            sc = jnp.where((c <= tok) & (c < t_new), sc, -jnp.inf)
            m_scr[...] = jnp.full((rows, 1), -jnp.inf, F32)
            l_scr[...] = jnp.zeros((rows, 1), F32)
            acc[...] = jnp.zeros((rows, w), F32)
            update(sc, vn_ref[0].astype(BF16))

    @pl.when(ph == 1)
    def _():
        col = lax.broadcasted_iota(jnp.int32, (rows, PAGE_SIZE), 1)
        for i in range(npg):
            page = g * npg + i
            dist = (past_len + tok) - (page * PAGE_SIZE + col)
            sc = (_dot_nt(qs, k_refs[i][0].astype(BF16)) * ATTN_SCALE
                  - slope * dist.astype(F32))
            picked = jnp.sum(jnp.where(bidx == page // per_blk, sel[...], 0.0),
                             axis=-1, keepdims=True) > 0.5
            update(jnp.where(picked, sc, -jnp.inf), v_refs[i][0].astype(BF16))

        @pl.when(last_g)
        def _():
            o = jnp.where(own_lanes, acc[...] / l_scr[...], 0.0)
            o_ref[0] = jnp.sum(o.reshape(B_HEADS, DEC_TOK, w), axis=0)


def moba_decode(qn, kn, vn, cache_k, cache_v, page_table, t_new):
    s, _, w = qn.shape
    n_pages = page_table.shape[1]
    past_len = n_pages * PAGE_SIZE
    assert past_len % MOBA_BLOCK == 0 and n_pages % DEC_PAGES == 0 and t_new <= DEC_TOK
    nb = past_len // MOBA_BLOCK
    rows = B_HEADS * DEC_TOK
    new = pl.BlockSpec((1, DEC_TOK, w), lambda b, ph, g, pt: (b, 0, 0))

    def k_spec(i):
        return pl.BlockSpec((1, PAGE_SIZE, w), lambda b, ph, g, pt: (pt[b, g * DEC_PAGES + i], 0, 0))

    def v_spec(i):
        return pl.BlockSpec((1, PAGE_SIZE, w),
                            lambda b, ph, g, pt: (pt[b, jnp.where(ph == 0, 0, g) * DEC_PAGES + i], 0, 0))

    grid_spec = pltpu.PrefetchScalarGridSpec(
        num_scalar_prefetch=1,
        grid=(s, 2, n_pages // DEC_PAGES),
        in_specs=[new, new, new] + [k_spec(i) for i in range(DEC_PAGES)] + [v_spec(i) for i in range(DEC_PAGES)],
        out_specs=new,
        scratch_shapes=[pltpu.VMEM((nb, w), F32), pltpu.VMEM((rows, nb), F32), pltpu.VMEM((rows, 1), F32),
                        pltpu.VMEM((rows, 1), F32), pltpu.VMEM((rows, w), F32)],
    )
    return pl.pallas_call(
        functools.partial(_moba_dec_kernel, t_new=t_new, past_len=past_len),
        grid_spec=grid_spec,
        out_shape=jax.ShapeDtypeStruct((s, DEC_TOK, w), F32),
        compiler_params=_cparams(("parallel", "arbitrary", "arbitrary")),
        name="moba_decode",
    )(page_table, qn, kn, vn, *([cache_k] * DEC_PAGES), *([cache_v] * DEC_PAGES))


IN0_SPLITS = ((0, A_COLS), (A_COLS, B_WIDTH), (A_COLS + B_WIDTH, B_WIDTH), (A_COLS + 2 * B_WIDTH, B_WIDTH))


def _pad_tokens(a, t_pad):
    return jnp.pad(a, ((0, 0), (0, t_pad - a.shape[1]), (0, 0)))


def kernel(x_prompt, x_sample, cache_l0_k, cache_l0_v, state_l0_wkv, state_l0_shift, state_l1_conv, page_table, norm0_mix_g, w_in0, mu_shift, w_lora_up, w0, a_lora_up, a0, g_lora_up, k_k, k_a, r_k, lnx_g, lnx_b, q_norm_g, k_norm_g, w_out0, norm0_ffn_g, ffn_w_gate, ffn_w_up, ffn_w_down, norm1_mix_g, conv_w_in, conv_dw_w, conv_dw_b, conv_ln_g, conv_ln_b, conv_w_out, norm1_ffn_g, moe_router, moe_w_gate, moe_w_up, moe_w_down):
    bp, tp, d = x_prompt.shape
    bs, ts, _ = x_sample.shape
    n_p, n_s = bp * tp, bs * DEC_TOK
    cast = lambda a: a.astype(BF16)
    w_in0_b, w_out0_b = cast(w_in0), cast(w_out0)
    ffn_g_b, ffn_u_b, ffn_d_b = cast(ffn_w_gate), cast(ffn_w_up), cast(ffn_w_down)
    conv_in_b, conv_out_b = cast(conv_w_in), cast(conv_w_out)
    moe_g_b, moe_u_b, moe_d_b = cast(moe_w_gate), cast(moe_w_up), cast(moe_w_down)
    rk = r_k.reshape(-1)
    rwkv_w = (mu_shift, w_lora_up, w0, a_lora_up, a0, g_lora_up, k_k, k_a, rk)

    xp = x_prompt.reshape(n_p, d)
    xs = _pad_tokens(x_sample, DEC_TOK).reshape(n_s, d)

    p_a, q, k, v_p = norm_proj(xp, norm0_mix_g, w_in0_b, IN0_SPLITS, 512)
    p_a3 = p_a.reshape(bp, tp, A_COLS)
    r_, lw_, kt_, vv_, kap_, b_, g_, bonus_ = rwkv_prep(
        p_a3, jnp.zeros((bp, A_COLS), F32), *rwkv_w, 256)
    y_p, wkv_prompt = rwkv_chunked(r_, lw_, kt_, vv_, kap_, b_,
                                   jnp.zeros((bp, A_HEADS, HEAD_DIM, HEAD_DIM), F32), RWKV_CHUNK, tp, bp,
                                   RWKV_PASSES_PREFILL)
    kn_p, km, q_t, k_b, v_t = moba_prep_t(q, k, v_p, q_norm_g, k_norm_g, bp)
    o_p = moba_prefill(q_t, k_b.reshape(bp, tp, B_WIDTH), v_t, km.reshape(bp, tp // MOBA_BLOCK, B_WIDTH))
    flat = lambda a: a.reshape(-1, a.shape[-1])
    xp = mix_out(xp, flat(y_p), flat(bonus_), flat(g_), flat(o_p), lnx_g, lnx_b, w_out0_b, 512)
    xp = ffn(xp, norm0_ffn_g, ffn_g_b, ffn_u_b, ffn_d_b, 512, D_FF // 2)
    shift_prompt = p_a3[:, -1]

    ps_a, q, k, v_s = norm_proj(xs, norm0_mix_g, w_in0_b, IN0_SPLITS, n_s)
    ps_a3 = ps_a.reshape(bs, DEC_TOK, A_COLS)
    r_, lw_, kt_, vv_, kap_, b_, g_, bonus_ = rwkv_prep(ps_a3, state_l0_shift, *rwkv_w, DEC_TOK)
    y_s, wkv_sample = rwkv_chunked(r_, lw_, kt_, vv_, kap_, b_, state_l0_wkv, DEC_TOK, ts, 2,
                                   RWKV_PASSES_DECODE)
    qn, kn_s, _ = moba_prep(q, k, q_norm_g, k_norm_g, n_s)
    tok3 = lambda a: a.reshape(bs, DEC_TOK, B_WIDTH)
    pool = cache_l0_k.shape[0]
    o_s = moba_decode(tok3(qn), tok3(kn_s), tok3(v_s), cache_l0_k.reshape(pool, PAGE_SIZE, B_WIDTH),
                      cache_l0_v.reshape(pool, PAGE_SIZE, B_WIDTH), page_table, ts)
    xs = mix_out(xs, flat(y_s), flat(bonus_), flat(g_), flat(o_s), lnx_g, lnx_b, w_out0_b, n_s)
    xs = ffn(xs, norm0_ffn_g, ffn_g_b, ffn_u_b, ffn_d_b, n_s, D_FF // 2)
    shift_sample = ps_a3[:, ts - 1]

    u_p = conv_in(xp, norm1_mix_g, conv_in_b, 512).reshape(bp, tp, d)
    xp = conv_dw(u_p, jnp.zeros((bp, HALO, d), F32), xp.reshape(bp, tp, d), conv_dw_w, conv_dw_b,
                 conv_ln_g, conv_ln_b, conv_out_b, 256).reshape(n_p, d)
    xp = moe_layer(xp, norm1_ffn_g, moe_router, moe_g_b, moe_u_b, moe_d_b, MOE_ROWS)
    conv_prompt = u_p[:, tp - (CONV_K - 1):]

    u_s = conv_in(xs, norm1_mix_g, conv_in_b, n_s).reshape(bs, DEC_TOK, d)
    init = jnp.pad(state_l1_conv, ((0, 0), (HALO - (CONV_K - 1), 0), (0, 0)))
    xs = conv_dw(u_s, init, xs.reshape(bs, DEC_TOK, d), conv_dw_w, conv_dw_b,
                 conv_ln_g, conv_ln_b, conv_out_b, DEC_TOK).reshape(n_s, d)
    xs = moe_layer(xs, norm1_ffn_g, moe_router, moe_g_b, moe_u_b, moe_d_b, LANES)
    conv_sample = jnp.concatenate([state_l1_conv, u_s[:, :ts]], axis=1)[:, -(CONV_K - 1):]

    heads = lambda a, b_, t_: a.reshape(b_, -1, B_WIDTH)[:, :t_].reshape(b_, t_, B_HEADS, HEAD_DIM)
    return (xp.reshape(bp, tp, d), xs.reshape(bs, DEC_TOK, d)[:, :ts],
            heads(kn_p, bp, tp), heads(v_p, bp, tp), heads(kn_s, bs, ts), heads(v_s, bs, ts),
            wkv_prompt, wkv_sample, shift_prompt, shift_sample, conv_prompt, conv_sample)
```

```python
import functools

import jax
import jax.numpy as jnp
from jax import lax
from jax.experimental import pallas as pl
from jax.experimental.pallas import tpu as pltpu

F32 = jnp.float32
BF16 = jnp.bfloat16

D_MODEL = 1024
HEAD_DIM = 64
A_WIDTH = 512
A_HEADS = 8
B_WIDTH = 512
B_HEADS = 8
LORA_DECAY = 64
LORA_ICLR = 64
LORA_GATE = 128
A_COLS = 3 * A_WIDTH + LORA_DECAY + LORA_ICLR + LORA_GATE
MOBA_BLOCK = 256
MOBA_TOPK = 3
PAGE_SIZE = 128
ATTN_SCALE = HEAD_DIM ** -0.5
D_FF = 2816
N_EXPERTS = 8
D_FF_EXPERT = 3584
CONV_K = 31
RMS_EPS = 1e-6
GN_EPS = 64e-5
LN_EPS = 1e-5

LANES = 128
VMEM_LIMIT = 56 * 1024 * 1024
RWKV_CHUNK = 64
MOE_ROWS = 512
HALO = 32
HI = lax.Precision.HIGHEST


def _cparams(sem):
    return pltpu.CompilerParams(dimension_semantics=sem, vmem_limit_bytes=VMEM_LIMIT)


def _rms(x, g):
    return x * lax.rsqrt(jnp.mean(x * x, axis=-1, keepdims=True) + RMS_EPS) * g


def _bdot(a, b):
    return jnp.dot(a.astype(BF16), b.astype(BF16), preferred_element_type=F32)


def _dot_nt(a, b, precision=None):
    return lax.dot_general(a, b, (((1,), (1,)), ((), ())), precision=precision,
                           preferred_element_type=F32)


def _seg_sum(x, bd):
    hi = x.astype(BF16)
    lo = (x - hi.astype(F32)).astype(BF16)
    return (jnp.dot(hi, bd, preferred_element_type=F32)
            + jnp.dot(lo, bd, preferred_element_type=F32))


def _block_diag_ones(n, seg):
    i = jnp.arange(n)
    return (i[:, None] // seg == i[None, :] // seg).astype(BF16)


def _norm_proj_kernel(x_ref, g_ref, w_ref, *out_refs, splits):
    h = _rms(x_ref[...], g_ref[...])
    if w_ref.dtype == F32:
        dot = functools.partial(jnp.dot, h, precision=HI, preferred_element_type=F32)
    else:
        dot = functools.partial(jnp.dot, h.astype(BF16), preferred_element_type=F32)
    for o_ref, (c0, cn) in zip(out_refs, splits):
        o_ref[...] = dot(w_ref[:, c0:c0 + cn])


def norm_proj(x, g, w_bf16, splits, tm):
    n, d = x.shape
    cols = w_bf16.shape[1]
    return pl.pallas_call(
        functools.partial(_norm_proj_kernel, splits=splits),
        grid=(n // tm,),
        in_specs=[pl.BlockSpec((tm, d), lambda i: (i, 0)),
                  pl.BlockSpec((1, d), lambda i: (0, 0)),
                  pl.BlockSpec((d, cols), lambda i: (0, 0))],
        out_specs=[pl.BlockSpec((tm, cn), lambda i: (i, 0)) for _, cn in splits],
        out_shape=[jax.ShapeDtypeStruct((n, cn), F32) for _, cn in splits],
        compiler_params=_cparams(("parallel",)),
        name="norm_proj",
    )(x, g.reshape(1, d), w_bf16)


def _rwkv_prep_kernel(p_ref, shift_ref, mu_ref, wl_ref, w0_ref, al_ref, a0_ref, gl_ref,
                      kk_ref, ka_ref, rk_ref, bd_ref,
                      r_o, lw_o, k_o, v_o, kap_o, b_o, g_o, bonus_o, carry):
    t = pl.program_id(1)

    @pl.when(t == 0)
    def _():
        carry[...] = shift_ref[0]

    p = p_ref[0]
    tt = p.shape[0]
    rolled = pltpu.roll(p, shift=1, axis=0)
    row = lax.broadcasted_iota(jnp.int32, p.shape, 0)
    prev = jnp.where(row == 0, carry[...], rolled)
    carry[...] = p[tt - 1:tt, :]
    xs = p + (prev - p) * mu_ref[...]
    aw = A_WIDTH
    r = xs[:, 0:aw]
    k = xs[:, aw:2 * aw]
    v = xs[:, 2 * aw:3 * aw]
    c0 = 3 * aw
    xw = xs[:, c0:c0 + LORA_DECAY]
    xa = xs[:, c0 + LORA_DECAY:c0 + LORA_DECAY + LORA_ICLR]
    xg = xs[:, c0 + LORA_DECAY + LORA_ICLR:]
    bd = bd_ref[...]

    z = w0_ref[...] + _bdot(jnp.tanh(xw), wl_ref[...])
    nz = -z
    softplus = jnp.maximum(nz, 0.0) + jnp.log(1.0 + jnp.exp(-jnp.abs(nz)))
    w_log = -softplus - 0.5
    lw_o[0] = -jnp.exp(w_log)
    a = jax.nn.sigmoid(a0_ref[...] + _bdot(xa, al_ref[...]))
    g_o[0] = _bdot(jax.nn.sigmoid(xg), gl_ref[...])
    kk = k * kk_ref[...]
    nrm = jnp.sqrt(_seg_sum(kk * kk, bd))
    kap = kk / jnp.maximum(nrm, 1e-12)
    k_h = k * (1.0 + (a - 1.0) * ka_ref[...])
    r_o[0] = r
    k_o[0] = k_h
    v_o[0] = v
    kap_o[0] = kap
    b_o[0] = kap * a
    bonus_o[0] = _seg_sum(r * k_h * rk_ref[...], bd) * v


def rwkv_prep(p_a, shift_prev, mu_shift, w_lora_up, w0, a_lora_up, a0, g_lora_up, k_k, k_a, r_k, tt):
    b, t, _ = p_a.shape
    row = lambda x: x.reshape(1, -1)
    const = lambda shape: pl.BlockSpec(shape, lambda i, j: (0,) * len(shape))
    tok = lambda w: pl.BlockSpec((1, tt, w), lambda i, j: (i, j, 0))
    outs = [jax.ShapeDtypeStruct((b, t, A_WIDTH), F32)] * 8
    return pl.pallas_call(
        _rwkv_prep_kernel,
        grid=(b, t // tt),
        in_specs=[tok(A_COLS),
                  pl.BlockSpec((1, 1, A_COLS), lambda i, j: (i, 0, 0)),
                  const((1, A_COLS)),
                  const((LORA_DECAY, A_WIDTH)), const((1, A_WIDTH)),
                  const((LORA_ICLR, A_WIDTH)), const((1, A_WIDTH)),
                  const((LORA_GATE, A_WIDTH)),
                  const((1, A_WIDTH)), const((1, A_WIDTH)), const((1, A_WIDTH)),
                  const((A_WIDTH, A_WIDTH))],
        out_specs=[tok(A_WIDTH)] * 8,
        out_shape=outs,
        scratch_shapes=[pltpu.VMEM((1, A_COLS), F32)],
        compiler_params=_cparams(("parallel", "arbitrary")),
        name="rwkv_prep",
    )(p_a, shift_prev.reshape(b, 1, A_COLS), row(mu_shift), w_lora_up.astype(BF16), row(w0),
      a_lora_up.astype(BF16), row(a0), g_lora_up.astype(BF16), row(k_k), row(k_a), row(r_k),
      _block_diag_ones(A_WIDTH, HEAD_DIM))


RWKV_PASSES_PREFILL = {"gram": 3, "inv": 1, "apply": 1, "state": 1}
RWKV_PASSES_DECODE = {"gram": 6, "inv": 6, "apply": 6, "state": 6}


def _split(x):
    hi = x.astype(BF16)
    return hi, (x - hi.astype(F32)).astype(BF16)


def _mm(a, b, passes, dims=(1, 0)):
    dn = (((dims[0],), (dims[1],)), ((), ()))
    dot = functools.partial(lax.dot_general, dimension_numbers=dn, preferred_element_type=F32)
    if passes == 6:
        return dot(a, b, precision=HI)
    if passes == 1:
        return dot(a.astype(BF16), b.astype(BF16))
    ah, al = _split(a)
    bh, bl = _split(b)
    return dot(ah, bh) + (dot(ah, bl) + dot(al, bh))


def _unit_lower_inverses(ls, n, c, passes):
    ri = lax.broadcasted_iota(jnp.int32, (n, n), 0)
    ci = lax.broadcasted_iota(jnp.int32, (n, n), 1)
    eye = (ri == ci).astype(F32)

    def same_block(bs):
        return (ri // bs) == (ci // bs)

    b0 = min(16, c)
    powers = [jnp.where(same_block(b0), -l, 0.0) for l in ls]
    invs = [eye + p for p in powers]
    span = 2
    while span < b0:
        powers = [_mm(p, p, passes) for p in powers]
        invs = [i + _mm(i, p, passes) for i, p in zip(invs, powers)]
        span *= 2
    bs = b0
    while bs < c:
        cross = same_block(2 * bs) & jnp.logical_not(same_block(bs))
        mids = [_mm(i, jnp.where(cross, l, 0.0), passes) for i, l in zip(invs, ls)]
        invs = [i - _mm(m, i, passes) for i, m in zip(invs, mids)]
        bs *= 2
    return invs


def _rwkv_chunk_kernel(r_ref, lw_ref, k_ref, v_ref, kap_ref, b_ref, s0_ref, y_ref, sout_ref, state,
                       *, chunk, t_valid, pw):
    ci = pl.program_id(1)
    c = chunk
    n = 2 * c
    hd = HEAD_DIM
    pairs = [(bi, j) for bi in range(r_ref.shape[0]) for j in range(A_HEADS // 2)]

    @pl.when(ci == 0)
    def _():
        z = jnp.zeros((hd, hd), F32)
        for idx, (bi, j) in enumerate(pairs):
            state[idx] = jnp.concatenate(
                [jnp.concatenate([s0_ref[bi, 2 * j], z], axis=1),
                 jnp.concatenate([z, s0_ref[bi, 2 * j + 1]], axis=1)], axis=0)

    lane = lax.broadcasted_iota(jnp.int32, (c, LANES), 1)
    trow = lax.broadcasted_iota(jnp.int32, (c, LANES), 0) + ci * c
    live = trow < t_valid
    m0 = lane < hd
    tri = (lax.broadcasted_iota(jnp.int32, (c, c), 0)
           >= lax.broadcasted_iota(jnp.int32, (c, c), 1)).astype(BF16)
    ri = lax.broadcasted_iota(jnp.int32, (n, n), 0)
    cj = lax.broadcasted_iota(jnp.int32, (n, n), 1)
    same = (ri // c) == (cj // c)
    strict = same & (ri > cj)
    incl = same & (ri >= cj)

    def stack(x):
        return jnp.concatenate([jnp.where(m0, x, 0.0), jnp.where(m0, 0.0, x)], axis=0)

    def cumulative(x):
        hi, lo = _split(x)
        lo2 = (x - hi.astype(F32) - lo.astype(F32)).astype(BF16)
        dot = functools.partial(jnp.dot, preferred_element_type=F32)
        return dot(tri, hi) + (dot(tri, lo) + dot(tri, lo2))

    each = lambda f, *cols: [f(*xs) for xs in zip(*cols)]
    sls = [(bi, slice(j * LANES, (j + 1) * LANES)) for bi, j in pairs]
    ld = lambda ref: [jnp.where(live, ref[bi, :, sl], 0.0) for bi, sl in sls]
    r = [r_ref[bi, :, sl] for bi, sl in sls]
    lw, k, v, kap, b = ld(lw_ref), ld(k_ref), ld(v_ref), ld(kap_ref), ld(b_ref)
    cum = each(cumulative, lw)
    total = [x[c - 1:c, :] for x in cum]
    ginv = [jnp.exp(-x) for x in cum]
    tail = each(lambda t, x: jnp.exp(t - x), total, cum)
    kap_s = each(lambda a, x, w: stack(a * jnp.exp(x - w)), kap, cum, lw)
    r_s = each(lambda a, x: stack(a * jnp.exp(x)), r, cum)
    b_s = each(lambda a, g: stack(a * g), b, ginv)
    k_s = each(lambda a, g: stack(a * g), k, ginv)
    v_s = each(stack, v)
    kbar_s = each(lambda a, t: stack(a * t), k, tail)
    bbar_s = each(lambda a, t: stack(a * t), b, tail)
    gram = lambda x, y: _mm(x, y, pw["gram"], (1, 1))
    l_b = each(lambda x, y: jnp.where(strict, gram(x, y), 0.0), kap_s, b_s)
    l_k = each(lambda x, y: jnp.where(strict, gram(x, y), 0.0), kap_s, k_s)
    a_qk = each(lambda x, y: jnp.where(incl, gram(x, y), 0.0), r_s, k_s)
    a_qb = each(lambda x, y: jnp.where(incl, gram(x, y), 0.0), r_s, b_s)
    inv = _unit_lower_inverses(l_b, n, c, pw["inv"])
    app = lambda x, y: _mm(x, y, pw["apply"])
    w_s = each(app, inv, kap_s)
    u0_s = each(app, inv, each(app, l_k, v_s))
    y0_s = each(app, a_qk, v_s)

    st = lambda x, y, dims=(1, 0): _mm(x, y, pw["state"], dims)
    s = [state[idx] for idx in range(len(pairs))]
    u_s = each(lambda w, s_, u0: st(w, s_, (1, 1)) + u0, w_s, s, u0_s)
    y_s = each(lambda r_, s_, y0, a, u: st(r_, s_, (1, 1)) + y0 - st(a, u), r_s, s, y0_s, a_qb, u_s)
    s_new = each(lambda s_, t, v_, kb, u, bb: s_ * jnp.exp(t) + st(v_, kb, (0, 0)) - st(u, bb, (0, 0)),
                 s, total, v_s, kbar_s, u_s, bbar_s)
    for idx, (bi, sl) in enumerate(sls):
        y_ref[bi, :, sl] = y_s[idx][0:c] + y_s[idx][c:n]
        state[idx] = s_new[idx]

    @pl.when(ci == pl.num_programs(1) - 1)
    def _():
        for idx, (bi, j) in enumerate(pairs):
            s = state[idx]
            sout_ref[bi, 2 * j] = s[0:hd, 0:hd]
            sout_ref[bi, 2 * j + 1] = s[hd:2 * hd, hd:2 * hd]


def rwkv_chunked(r, lw, k, v, kap, b, s0, chunk, t_valid, bb, passes):
    bsz, t, w = r.shape
    tok = pl.BlockSpec((bb, chunk, w), lambda i, c: (i, c, 0))
    st = pl.BlockSpec((bb, A_HEADS, HEAD_DIM, HEAD_DIM), lambda i, c: (i, 0, 0, 0))
    y, s_out = pl.pallas_call(
        functools.partial(_rwkv_chunk_kernel, chunk=chunk, t_valid=t_valid, pw=passes),
        grid=(bsz // bb, t // chunk),
        in_specs=[tok] * 6 + [st],
        out_specs=[tok, st],
        out_shape=[jax.ShapeDtypeStruct((bsz, t, w), F32),
                   jax.ShapeDtypeStruct((bsz, A_HEADS, HEAD_DIM, HEAD_DIM), F32)],
        scratch_shapes=[pltpu.VMEM((bb * A_HEADS // 2, LANES, LANES), F32)],
        compiler_params=_cparams(("parallel", "arbitrary")),
        name="rwkv_chunk",
    )(r, lw, k, v, kap, b, s0)
    return y, s_out


def _mix_out_kernel(x_ref, y_ref, bonus_ref, g_ref, o_ref, lng_ref, lnb_ref, bd_ref, w_ref, out_ref):
    y = y_ref[...]
    bd = bd_ref[...]
    mu = _seg_sum(y, bd) * (1.0 / HEAD_DIM)
    d = y - mu
    var = _seg_sum(d * d, bd) * (1.0 / HEAD_DIM)
    yn = d * lax.rsqrt(var + GN_EPS) * lng_ref[...] + lnb_ref[...]
    ya = (yn + bonus_ref[...]) * g_ref[...]
    acc = jnp.dot(ya.astype(BF16), w_ref[0:A_WIDTH, :], preferred_element_type=F32)
    acc += jnp.dot(o_ref[...].astype(BF16), w_ref[A_WIDTH:, :], preferred_element_type=F32)
    out_ref[...] = x_ref[...] + acc


def mix_out(x, y, bonus, g, o, lnx_g, lnx_b, w_out_bf16, tm):
    n, d = x.shape
    tok = lambda w: pl.BlockSpec((tm, w), lambda i: (i, 0))
    const = lambda shape: pl.BlockSpec(shape, lambda i: (0,) * len(shape))
    return pl.pallas_call(
        _mix_out_kernel,
        grid=(n // tm,),
        in_specs=[tok(d), tok(A_WIDTH), tok(A_WIDTH), tok(A_WIDTH), tok(B_WIDTH),
                  const((1, A_WIDTH)), const((1, A_WIDTH)), const((A_WIDTH, A_WIDTH)),
                  const((A_WIDTH + B_WIDTH, d))],
        out_specs=tok(d),
        out_shape=jax.ShapeDtypeStruct((n, d), F32),
        compiler_params=_cparams(("parallel",)),
        name="mix_out",
    )(x, y, bonus, g, o, lnx_g.reshape(1, -1), lnx_b.reshape(1, -1),
      _block_diag_ones(A_WIDTH, HEAD_DIM), w_out_bf16)


def _ffn_kernel(x_ref, g_ref, wg_ref, wu_ref, wd_ref, out_ref, h_scr, acc):
    j = pl.program_id(1)

    @pl.when(j == 0)
    def _():
        h_scr[...] = _rms(x_ref[...], g_ref[...]).astype(BF16)
        acc[...] = jnp.zeros_like(acc)

    h = h_scr[...]
    gate = jnp.dot(h, wg_ref[...], preferred_element_type=F32)
    up = jnp.dot(h, wu_ref[...], preferred_element_type=F32)
    act = (jax.nn.silu(gate) * up).astype(BF16)
    acc[...] += jnp.dot(act, wd_ref[...], preferred_element_type=F32)

    @pl.when(j == pl.num_programs(1) - 1)
    def _():
        out_ref[...] = x_ref[...] + acc[...]


def ffn(x, g, wg, wu, wd, tm, tf):
    n, d = x.shape
    f = wg.shape[1]
    return pl.pallas_call(
        _ffn_kernel,
        grid=(n // tm, f // tf),
        in_specs=[pl.BlockSpec((tm, d), lambda i, j: (i, 0)),
                  pl.BlockSpec((1, d), lambda i, j: (0, 0)),
                  pl.BlockSpec((d, tf), lambda i, j: (0, j)),
                  pl.BlockSpec((d, tf), lambda i, j: (0, j)),
                  pl.BlockSpec((tf, d), lambda i, j: (j, 0))],
        out_specs=pl.BlockSpec((tm, d), lambda i, j: (i, 0)),
        out_shape=jax.ShapeDtypeStruct((n, d), F32),
        scratch_shapes=[pltpu.VMEM((tm, d), BF16), pltpu.VMEM((tm, d), F32)],
        compiler_params=_cparams(("parallel", "arbitrary")),
        name="ffn",
    )(x, g.reshape(1, d), wg, wu, wd)


def _conv_in_kernel(x_ref, g_ref, w_ref, u_ref):
    h = _rms(x_ref[...], g_ref[...]).astype(BF16)
    d = u_ref.shape[1]
    a = jnp.dot(h, w_ref[:, 0:d], preferred_element_type=F32)
    gate = jnp.dot(h, w_ref[:, d:2 * d], preferred_element_type=F32)
    u_ref[...] = a * jax.nn.sigmoid(gate)


def conv_in(x, g, w_bf16, tm):
    n, d = x.shape
    return pl.pallas_call(
        _conv_in_kernel,
        grid=(n // tm,),
        in_specs=[pl.BlockSpec((tm, d), lambda i: (i, 0)),
                  pl.BlockSpec((1, d), lambda i: (0, 0)),
                  pl.BlockSpec((d, 2 * d), lambda i: (0, 0))],
        out_specs=pl.BlockSpec((tm, d), lambda i: (i, 0)),
        out_shape=jax.ShapeDtypeStruct((n, d), F32),
        compiler_params=_cparams(("parallel",)),
        name="conv_in",
    )(x, g.reshape(1, d), w_bf16)


def _conv_dw_kernel(u_ref, init_ref, x_ref, dw_ref, db_ref, lg_ref, lb_ref, w_ref, out_ref, ctx):
    t = pl.program_id(1)
    tt = u_ref.shape[1]

    @pl.when(t == 0)
    def _():
        ctx[0:HALO, :] = init_ref[0]

    ctx[HALO:HALO + tt, :] = u_ref[0]
    acc = jnp.zeros((tt, D_MODEL), F32) + db_ref[...]
    for j in range(CONV_K):
        acc = acc + ctx[pl.ds(HALO - (CONV_K - 1) + j, tt), :] * dw_ref[j:j + 1, :]
    keep = ctx[tt:tt + HALO, :]
    ctx[0:HALO, :] = keep
    mu = jnp.mean(acc, axis=-1, keepdims=True)
    d = acc - mu
    var = jnp.mean(d * d, axis=-1, keepdims=True)
    cf = d * lax.rsqrt(var + LN_EPS) * lg_ref[...] + lb_ref[...]
    out_ref[0] = x_ref[0] + jnp.dot(jax.nn.silu(cf).astype(BF16), w_ref[...], preferred_element_type=F32)


def conv_dw(u, init, x, dw_w, dw_b, ln_g, ln_b, w_out_bf16, tt):
    b, t, d = u.shape
    dw = jnp.pad(dw_w, ((0, HALO - CONV_K), (0, 0)))
    tok = pl.BlockSpec((1, tt, d), lambda i, j: (i, j, 0))
    const = lambda shape: pl.BlockSpec(shape, lambda i, j: (0,) * len(shape))
    return pl.pallas_call(
        _conv_dw_kernel,
        grid=(b, t // tt),
        in_specs=[tok, pl.BlockSpec((1, HALO, d), lambda i, j: (i, 0, 0)), tok,
                  const((HALO, d)), const((1, d)), const((1, d)), const((1, d)), const((d, d))],
        out_specs=tok,
        out_shape=jax.ShapeDtypeStruct((b, t, d), F32),
        scratch_shapes=[pltpu.VMEM((HALO + tt, d), F32)],
        compiler_params=_cparams(("parallel", "arbitrary")),
        name="conv_dw",
    )(u, init, x, dw, dw_b.reshape(1, d), ln_g.reshape(1, d), ln_b.reshape(1, d), w_out_bf16)


def _router_kernel(x_ref, g_ref, wr_ref, idx_ref, pw_ref):
    h = _rms(x_ref[...], g_ref[...])
    logits = _bdot(h, wr_ref[...])
    lane = lax.broadcasted_iota(jnp.int32, logits.shape, 1)
    valid = lane < N_EXPERTS
    logits = jnp.where(valid, logits, -jnp.inf)
    e = jnp.exp(logits - jnp.max(logits, axis=-1, keepdims=True))
    probs = jnp.where(valid, e / jnp.sum(e, axis=-1, keepdims=True), -1.0)
    m1 = jnp.max(probs, axis=-1, keepdims=True)
    i1 = jnp.min(jnp.where(probs == m1, lane, LANES), axis=-1, keepdims=True)
    rest = jnp.where(lane == i1, -1.0, probs)
    m2 = jnp.max(rest, axis=-1, keepdims=True)
    i2 = jnp.min(jnp.where(rest == m2, lane, LANES), axis=-1, keepdims=True)
    den = m1 + m2
    idx_ref[...] = jnp.where(lane == 0, i1, jnp.where(lane == 1, i2, 0))
    pw_ref[...] = jnp.where(lane == 0, m1 / den, jnp.where(lane == 1, m2 / den, 0.0))


def router(x, g, w_router, tm):
    n, d = x.shape
    wr = jnp.pad(w_router, ((0, 0), (0, LANES - N_EXPERTS))).astype(BF16)
    return pl.pallas_call(
        _router_kernel,
        grid=(n // tm,),
        in_specs=[pl.BlockSpec((tm, d), lambda i: (i, 0)),
                  pl.BlockSpec((1, d), lambda i: (0, 0)),
                  pl.BlockSpec((d, LANES), lambda i: (0, 0))],
        out_specs=[pl.BlockSpec((tm, LANES), lambda i: (i, 0))] * 2,
        out_shape=[jax.ShapeDtypeStruct((n, LANES), jnp.int32), jax.ShapeDtypeStruct((n, LANES), F32)],
        compiler_params=_cparams(("parallel",)),
        name="moe_router",
    )(x, g.reshape(1, d), wr)


def _moe_kernel(be_ref, nu_ref, tok_ref, dst_ref,
                x_hbm, g_ref, p_ref, wg_ref, wu_ref, wd_ref,
                y_hbm, xg, h_scr, acc, sem_in, sem_out, *, n_dst):
    i = pl.program_id(0)
    j = pl.program_id(1)
    m = xg.shape[0]
    valid = i < nu_ref[0]
    base = i * m

    def row_in(r, tok):
        return pltpu.make_async_copy(x_hbm.at[pl.ds(tok, 1), :], xg.at[pl.ds(r, 1), :], sem_in)

    def row_out(r, dst):
        return pltpu.make_async_copy(xg.at[pl.ds(r, 1), :], y_hbm.at[pl.ds(dst, 1), :], sem_out)

    @pl.when(valid & (j == 0))
    def _():
        def start(r, c):
            row_in(r, tok_ref[base + r]).start()
            return c

        def wait(r, c):
            row_in(r, 0).wait()
            return c

        lax.fori_loop(0, m, start, 0)
        lax.fori_loop(0, m, wait, 0)
        h_scr[...] = _rms(xg[...], g_ref[...]).astype(BF16)
        acc[...] = jnp.zeros_like(acc)

    @pl.when(valid)
    def _():
        h = h_scr[...]
        gate = jnp.dot(h, wg_ref[0], preferred_element_type=F32)
        up = jnp.dot(h, wu_ref[0], preferred_element_type=F32)
        act = (jax.nn.silu(gate) * up).astype(BF16)
        acc[...] += jnp.dot(act, wd_ref[0], preferred_element_type=F32)

    @pl.when(valid & (j == pl.num_programs(1) - 1))
    def _():
        xg[...] = acc[...] * p_ref[...]

        def start(r, c):
            dst = dst_ref[base + r]

            @pl.when(dst < n_dst)
            def _():
                row_out(r, dst).start()
            return c

        def wait(r, c):
            dst = dst_ref[base + r]

            @pl.when(dst < n_dst)
            def _():
                row_out(r, 0).wait()
            return c

        lax.fori_loop(0, m, start, 0)
        lax.fori_loop(0, m, wait, 0)


def moe_experts(x, g, block_e, n_used, buf_tok, buf_dst, buf_p, wg, wu, wd, m, tf):
    n, d = x.shape
    n_blocks = block_e.shape[0]
    nj = D_FF_EXPERT // tf

    def jj(i, j, nu):
        return jnp.where(i < nu[0], j, nj - 1)

    grid_spec = pltpu.PrefetchScalarGridSpec(
        num_scalar_prefetch=4,
        grid=(n_blocks, nj),
        in_specs=[pl.BlockSpec(memory_space=pl.ANY),
                  pl.BlockSpec((1, d), lambda i, j, be, nu, tk, ds: (0, 0)),
                  pl.BlockSpec((m, 1), lambda i, j, be, nu, tk, ds: (i, 0)),
                  pl.BlockSpec((1, d, tf), lambda i, j, be, nu, tk, ds: (be[i], 0, jj(i, j, nu))),
                  pl.BlockSpec((1, d, tf), lambda i, j, be, nu, tk, ds: (be[i], 0, jj(i, j, nu))),
                  pl.BlockSpec((1, tf, d), lambda i, j, be, nu, tk, ds: (be[i], jj(i, j, nu), 0))],
        out_specs=pl.BlockSpec(memory_space=pl.ANY),
        scratch_shapes=[pltpu.VMEM((m, d), F32), pltpu.VMEM((m, d), BF16), pltpu.VMEM((m, d), F32),
                        pltpu.SemaphoreType.DMA, pltpu.SemaphoreType.DMA],
    )
    return pl.pallas_call(
        functools.partial(_moe_kernel, n_dst=2 * n),
        grid_spec=grid_spec,
        out_shape=jax.ShapeDtypeStruct((2 * n, d), F32),
        compiler_params=_cparams(("arbitrary", "arbitrary")),
        name="moe_experts",
    )(block_e, n_used, buf_tok, buf_dst, x, g.reshape(1, d), buf_p, wg, wu, wd)


def _combine_kernel(x_ref, a_ref, b_ref, o_ref):
    o_ref[...] = x_ref[...] + (a_ref[0] + b_ref[0])


def moe_combine(x, y2, tm):
    n, d = x.shape
    y3 = y2.reshape(2, n, d)
    return pl.pallas_call(
        _combine_kernel,
        grid=(n // tm,),
        in_specs=[pl.BlockSpec((tm, d), lambda i: (i, 0)),
                  pl.BlockSpec((1, tm, d), lambda i: (0, i, 0)),
                  pl.BlockSpec((1, tm, d), lambda i: (1, i, 0))],
        out_specs=pl.BlockSpec((tm, d), lambda i: (i, 0)),
        out_shape=jax.ShapeDtypeStruct((n, d), F32),
        compiler_params=_cparams(("parallel",)),
        name="moe_combine",
    )(x, y3, y3)


def moe_layer(x, g, w_router, wg, wu, wd, m):
    n, d = x.shape
    idx, pw = router(x, g, w_router, min(n, 512))
    flat_e = idx[:, :2].reshape(-1)
    flat_p = pw[:, :2].reshape(-1)
    onehot = (flat_e[:, None] == jnp.arange(N_EXPERTS, dtype=jnp.int32)[None, :]).astype(jnp.int32)
    rank = jnp.sum((jnp.cumsum(onehot, axis=0) - onehot) * onehot, axis=1)
    counts = jnp.sum(onehot, axis=0)
    padded = (counts + m - 1) // m * m
    pend = jnp.cumsum(padded)
    slot = (pend - padded)[flat_e] + rank
    n_blocks = (2 * n) // m + N_EXPERTS
    n_slots = n_blocks * m
    owner = jnp.zeros((n_slots,), jnp.int32).at[slot].set(jnp.arange(1, 2 * n + 1, dtype=jnp.int32))
    pair = jnp.maximum(owner - 1, 0)
    buf_tok = pair // 2
    buf_dst = jnp.where(owner > 0, (pair % 2) * n + pair // 2, 2 * n)
    buf_p = jnp.where(owner > 0, flat_p[pair], 0.0).reshape(n_slots, 1)
    block_e = jnp.minimum(jnp.searchsorted(pend, jnp.arange(n_blocks, dtype=jnp.int32) * m, side='right'),
                          N_EXPERTS - 1).astype(jnp.int32)
    n_used = (pend[-1] // m).astype(jnp.int32).reshape(1)
    y2 = moe_experts(x, g, block_e, n_used, buf_tok, buf_dst, buf_p, wg, wu, wd, m, 512)
    return moe_combine(x, y2, min(n, 512))


def _moba_prep_kernel(q_ref, k_ref, qg_ref, kg_ref, bd_ref, qn_ref, kn_ref, km_ref):
    bd = bd_ref[...]

    def head_norm(x, g):
        ms = _seg_sum(x * x, bd) * (1.0 / HEAD_DIM)
        return x * lax.rsqrt(ms + RMS_EPS) * g

    qn_ref[...] = head_norm(q_ref[...], qg_ref[...])
    kn = head_norm(k_ref[...], kg_ref[...])
    kn_ref[...] = kn
    km_ref[0] = jnp.mean(kn, axis=0, keepdims=True)


def moba_prep(q, k, q_norm_g, k_norm_g, tm):
    n, w = q.shape
    tok = pl.BlockSpec((tm, w), lambda i: (i, 0))
    const = lambda shape: pl.BlockSpec(shape, lambda i: (0,) * len(shape))
    return pl.pallas_call(
        _moba_prep_kernel,
        grid=(n // tm,),
        in_specs=[tok, tok, const((1, w)), const((1, w)), const((w, w))],
        out_specs=[tok, tok, pl.BlockSpec((1, 1, w), lambda i: (i, 0, 0))],
        out_shape=[jax.ShapeDtypeStruct((n, w), F32), jax.ShapeDtypeStruct((n, w), F32),
                   jax.ShapeDtypeStruct((n // tm, 1, w), F32)],
        compiler_params=_cparams(("parallel",)),
        name="moba_prep",
    )(q, k, jnp.tile(q_norm_g, B_HEADS).reshape(1, w), jnp.tile(k_norm_g, B_HEADS).reshape(1, w),
      _block_diag_ones(w, HEAD_DIM))


def _top_blocks(gate, bidx, nb):
    chosen = jnp.zeros(gate.shape, F32)
    for _ in range(MOBA_TOPK):
        mx = jnp.max(gate, axis=-1, keepdims=True)
        first = jnp.min(jnp.where(gate == mx, bidx, nb), axis=-1, keepdims=True)
        chosen = jnp.where((bidx == first) & (mx > -jnp.inf), 1.0, chosen)
        gate = jnp.where(bidx == first, -jnp.inf, gate)
    return chosen


def _moba_prep_t_kernel(q_ref, k_ref, v_ref, qg_ref, kg_ref, bd_ref, kn_ref, km_ref, qt_ref, kb_ref, vt_ref):
    bd = bd_ref[...]

    def head_norm(x, g):
        ms = _seg_sum(x * x, bd) * (1.0 / HEAD_DIM)
        return x * lax.rsqrt(ms + RMS_EPS) * g

    qn = head_norm(q_ref[...], qg_ref[...])
    kn = head_norm(k_ref[...], kg_ref[...])
    kn_ref[...] = kn
    kb_ref[...] = kn.astype(BF16)
    km_ref[0] = jnp.mean(kn, axis=0, keepdims=True)
    qt_ref[0] = (qn * ATTN_SCALE).T.astype(BF16)
    vt_ref[0] = v_ref[...].T.astype(BF16)


def moba_prep_t(q, k, v, q_norm_g, k_norm_g, bsz):
    n, w = q.shape
    tm = MOBA_BLOCK
    nblk = n // bsz // tm
    tok = pl.BlockSpec((tm, w), lambda i: (i, 0))
    feat = pl.BlockSpec((1, w, tm), lambda i: (i // nblk, 0, i % nblk))
    const = lambda shape: pl.BlockSpec(shape, lambda i: (0,) * len(shape))
    return pl.pallas_call(
        _moba_prep_t_kernel,
        grid=(n // tm,),
        in_specs=[tok, tok, tok, const((1, w)), const((1, w)), const((w, w))],
        out_specs=[tok, pl.BlockSpec((1, 1, w), lambda i: (i, 0, 0)), feat, tok, feat],
        out_shape=[jax.ShapeDtypeStruct((n, w), F32), jax.ShapeDtypeStruct((n // tm, 1, w), F32),
                   jax.ShapeDtypeStruct((bsz, w, n // bsz), BF16), jax.ShapeDtypeStruct((n, w), BF16),
                   jax.ShapeDtypeStruct((bsz, w, n // bsz), BF16)],
        compiler_params=_cparams(("parallel",)),
        name="moba_prep_t",
    )(q, k, v, jnp.tile(q_norm_g, B_HEADS).reshape(1, w), jnp.tile(k_norm_g, B_HEADS).reshape(1, w),
      _block_diag_ones(w, HEAD_DIM))


def _moba_attn_kernel(qi_ref, ki_ref, qt_ref, k_ref, vt_ref, km_ref, o_ref,
                      qz, sel, bias, s_scr, m_scr, l_scr, acc):
    s = pl.program_id(1)
    qi = qi_ref[s]
    ki = ki_ref[s]
    blk = MOBA_BLOCK
    nb = km_ref.shape[1]
    hd = HEAD_DIM
    is_own = ki == qi
    key_i = lax.broadcasted_iota(jnp.int32, (blk, blk), 0)
    qry_i = lax.broadcasted_iota(jnp.int32, (blk, blk), 1)
    rel = key_i - qry_i
    blk_i = lax.broadcasted_iota(jnp.int32, (nb, blk), 0)
    step_off = jnp.full((1, 1), (qi - ki) * blk, jnp.int32).astype(F32)

    slopes = [2.0 ** (-8.0 * (h + 1) / B_HEADS) for h in range(B_HEADS)]

    def scores(h, own):
        hp = h // 2
        raw = jnp.dot(k_ref[0, :, hp * LANES:(hp + 1) * LANES], qz[h], preferred_element_type=F32)
        sp = raw + bias[h]
        if own:
            sp = jnp.where(rel <= 0, sp, -jnp.inf)
        s_scr[h] = sp
        return jnp.max(sp, axis=0, keepdims=True)

    def attend(h, own, top):
        if own:
            m_new = top
            p = jnp.exp(s_scr[h] - m_new)
            l_scr[h] = jnp.sum(p, axis=0, keepdims=True)
        else:
            live = sel[h, pl.ds(ki, 1), :] > 0.5
            off = slopes[h] * step_off
            m_prev = m_scr[h]
            m_new = jnp.maximum(m_prev, jnp.where(live, top - off, -jnp.inf))
            alpha = jnp.exp(m_prev - m_new)
            p = jnp.exp(s_scr[h] - jnp.where(live, m_new + off, jnp.inf))
            l_scr[h] = alpha * l_scr[h] + jnp.sum(p, axis=0, keepdims=True)
        pv = jnp.dot(vt_ref[0, h * hd:(h + 1) * hd, :], p.astype(BF16), preferred_element_type=F32)
        acc[h] = pv if own else alpha * acc[h] + pv
        m_scr[h] = m_new

    def all_heads(own):
        tops = [scores(h, own) for h in range(B_HEADS)]
        for h in range(B_HEADS):
            attend(h, own, tops[h])

    @pl.when(is_own)
    def _():
        frow = lax.broadcasted_iota(jnp.int32, (LANES, blk), 0)
        for h in range(B_HEADS):
            hp = h // 2
            qt = qt_ref[0, hp * LANES:(hp + 1) * LANES, :]
            mine = (frow < hd) if h % 2 == 0 else (frow >= hd)
            qh = jnp.where(mine, qt, jnp.zeros_like(qt))
            qz[h] = qh
            gate = jnp.dot(km_ref[0, :, hp * LANES:(hp + 1) * LANES].astype(BF16), qh,
                           preferred_element_type=F32)
            gate = jnp.where(blk_i < qi, gate, -jnp.inf)
            chosen = jnp.zeros((nb, blk), F32)
            for _ in range(MOBA_TOPK):
                mx = jnp.max(gate, axis=0, keepdims=True)
                first = jnp.min(jnp.where(gate == mx, blk_i, nb), axis=0, keepdims=True)
                chosen = jnp.where((blk_i == first) & (mx > -jnp.inf), 1.0, chosen)
                gate = jnp.where(blk_i == first, -jnp.inf, gate)
            sel[h] = chosen
            bias[h] = rel.astype(F32) * slopes[h]
        all_heads(True)

    @pl.when(jnp.logical_not(is_own))
    def _():
        all_heads(False)

    @pl.when((ki == qi - 1) | (qi == 0))
    def _():
        for hp in range(B_HEADS // 2):
            pair = jnp.concatenate([acc[2 * hp] / l_scr[2 * hp], acc[2 * hp + 1] / l_scr[2 * hp + 1]], axis=0)
            o_ref[0, :, hp * LANES:(hp + 1) * LANES] = pair.T


def moba_prefill(qt, kb, vt, km):
    b, w, t = qt.shape
    blk = MOBA_BLOCK
    nb = t // blk
    pairs = [(qi, ki) for qi in range(nb) for ki in [qi] + list(range(qi))]
    qi_tab = jnp.array([p[0] for p in pairs], jnp.int32)
    ki_tab = jnp.array([p[1] for p in pairs], jnp.int32)
    grid_spec = pltpu.PrefetchScalarGridSpec(
        num_scalar_prefetch=2,
        grid=(b, len(pairs)),
        in_specs=[pl.BlockSpec((1, w, blk), lambda i, s, qt_, kt_: (i, 0, qt_[s])),
                  pl.BlockSpec((1, blk, w), lambda i, s, qt_, kt_: (i, kt_[s], 0)),
                  pl.BlockSpec((1, w, blk), lambda i, s, qt_, kt_: (i, 0, kt_[s])),
                  pl.BlockSpec((1, nb, w), lambda i, s, qt_, kt_: (i, 0, 0))],
        out_specs=pl.BlockSpec((1, blk, w), lambda i, s, qt_, kt_: (i, qt_[s], 0)),
        scratch_shapes=[pltpu.VMEM((B_HEADS, LANES, blk), BF16), pltpu.VMEM((B_HEADS, nb, blk), F32),
                        pltpu.VMEM((B_HEADS, blk, blk), F32), pltpu.VMEM((B_HEADS, blk, blk), F32),
                        pltpu.VMEM((B_HEADS, 1, blk), F32),
                        pltpu.VMEM((B_HEADS, 1, blk), F32), pltpu.VMEM((B_HEADS, HEAD_DIM, blk), F32)],
    )
    return pl.pallas_call(
        _moba_attn_kernel,
        grid_spec=grid_spec,
        out_shape=jax.ShapeDtypeStruct((b, t, w), F32),
        compiler_params=_cparams(("parallel", "arbitrary")),
        name="moba_prefill",
    )(qi_tab, ki_tab, qt, kb, vt, km)


DEC_TOK = 8
DEC_PAGES = 8


def _moba_dec_kernel(pt_ref, q_ref, kn_ref, vn_ref, *refs, t_new, past_len):
    npg = DEC_PAGES
    k_refs, v_refs = refs[:npg], refs[npg:2 * npg]
    o_ref, kc, km_scr, sel, s_scr, m_scr, l_scr, acc = refs[2 * npg:]
    ph = pl.program_id(1)
    g = pl.program_id(2)
    last_g = g == pl.num_programs(2) - 1
    nh = B_HEADS
    rows = nh * DEC_TOK
    w = B_WIDTH
    per_blk = MOBA_BLOCK // PAGE_SIZE
    nb = km_scr.shape[0]

    qt = jnp.concatenate([q_ref[0]] * nh, axis=0)
    own_lanes = (lax.broadcasted_iota(jnp.int32, (rows, w), 1) // HEAD_DIM
                 == lax.broadcasted_iota(jnp.int32, (rows, w), 0) // DEC_TOK)
    qf = jnp.where(own_lanes, qt, 0.0)
    qb = (qf * ATTN_SCALE).astype(BF16)
    rcol = lax.broadcasted_iota(jnp.int32, (rows, 1), 0)
    tok = rcol % DEC_TOK
    slope = jnp.exp2(-8.0 * (rcol // DEC_TOK + 1).astype(F32) / nh)
    bidx = lax.broadcasted_iota(jnp.int32, (rows, nb), 1)

    @pl.when(ph == 0)
    def _():
        for i in range(0, npg, per_blk):
            tot = jnp.zeros((1, w), F32)
            for u in range(per_blk):
                kp = k_refs[i + u][0]
                kc[g * npg + i + u] = kp.astype(BF16)
                tot = tot + jnp.sum(kp, axis=0, keepdims=True)
            km_scr[pl.ds(g * (npg // per_blk) + i // per_blk, 1), :] = tot * (1.0 / MOBA_BLOCK)

        @pl.when(last_g)
        def _():
            gate = _dot_nt(qf, km_scr[...], HI)
            sel[...] = _top_blocks(gate, bidx, nb)
            c = lax.broadcasted_iota(jnp.int32, (rows, DEC_TOK), 1)
            sc = _dot_nt(qb, kn_ref[0].astype(BF16)) - slope * (tok - c).astype(F32)
            sc = jnp.where((c <= tok) & (c < t_new), sc, -jnp.inf)
            m_new = jnp.max(sc, axis=-1, keepdims=True)
            p = jnp.exp(sc - m_new)
            m_scr[...] = m_new
            l_scr[...] = jnp.sum(p, axis=-1, keepdims=True)
            acc[...] = jnp.dot(p.astype(BF16), vn_ref[0].astype(BF16), preferred_element_type=F32)

    @pl.when(ph == 1)
    def _():
        key_bias = slope * lax.broadcasted_iota(jnp.int32, (rows, PAGE_SIZE), 1).astype(F32)
        tops = []
        for i in range(npg):
            page = g * npg + i
            picked = jnp.sum(jnp.where(bidx == page // per_blk, sel[...], 0.0),
                             axis=-1, keepdims=True) > 0.5
            dist0 = (past_len + tok - page * PAGE_SIZE).astype(F32)
            sc = _dot_nt(qb, kc[page]) + key_bias + jnp.where(picked, -slope * dist0, -jnp.inf)
            s_scr[i] = sc
            tops.append(jnp.max(sc, axis=-1, keepdims=True))
        m_prev = m_scr[...]
        m_new = functools.reduce(jnp.maximum, tops, m_prev)
        alpha = jnp.exp(m_prev - m_new)
        lsum = jnp.zeros((rows, 1), F32)
        pv = jnp.zeros((rows, w), F32)
        for i in range(npg):
            p = jnp.exp(s_scr[i] - m_new)
            lsum = lsum + jnp.sum(p, axis=-1, keepdims=True)
            pv = pv + jnp.dot(p.astype(BF16), v_refs[i][0].astype(BF16), preferred_element_type=F32)
        l_scr[...] = alpha * l_scr[...] + lsum
        acc[...] = alpha * acc[...] + pv
        m_scr[...] = m_new

        @pl.when(last_g)
        def _():
            o = jnp.where(own_lanes, acc[...] / l_scr[...], 0.0)
            o_ref[0] = jnp.sum(o.reshape(nh, DEC_TOK, w), axis=0)


def moba_decode(qn, kn, vn, cache_k, cache_v, page_table, t_new):
    s, _, w = qn.shape
    n_pages = page_table.shape[1]
    past_len = n_pages * PAGE_SIZE
    assert past_len % MOBA_BLOCK == 0 and n_pages % DEC_PAGES == 0 and t_new <= DEC_TOK
    nb = past_len // MOBA_BLOCK
    rows = B_HEADS * DEC_TOK
    new = pl.BlockSpec((1, DEC_TOK, w), lambda b, ph, g, pt: (b, 0, 0))
    last = n_pages // DEC_PAGES - 1

    def k_spec(i):
        return pl.BlockSpec((1, PAGE_SIZE, w),
                            lambda b, ph, g, pt: (pt[b, jnp.where(ph == 0, g, last) * DEC_PAGES + i], 0, 0))

    def v_spec(i):
        return pl.BlockSpec((1, PAGE_SIZE, w),
                            lambda b, ph, g, pt: (pt[b, jnp.where(ph == 0, 0, g) * DEC_PAGES + i], 0, 0))

    grid_spec = pltpu.PrefetchScalarGridSpec(
        num_scalar_prefetch=1,
        grid=(s, 2, n_pages // DEC_PAGES),
        in_specs=[new, new, new] + [k_spec(i) for i in range(DEC_PAGES)] + [v_spec(i) for i in range(DEC_PAGES)],
        out_specs=new,
        scratch_shapes=[pltpu.VMEM((n_pages, PAGE_SIZE, w), BF16), pltpu.VMEM((nb, w), F32),
                        pltpu.VMEM((rows, nb), F32), pltpu.VMEM((DEC_PAGES, rows, PAGE_SIZE), F32),
                        pltpu.VMEM((rows, 1), F32), pltpu.VMEM((rows, 1), F32), pltpu.VMEM((rows, w), F32)],
    )
    return pl.pallas_call(
        functools.partial(_moba_dec_kernel, t_new=t_new, past_len=past_len),
        grid_spec=grid_spec,
        out_shape=jax.ShapeDtypeStruct((s, DEC_TOK, w), F32),
        compiler_params=_cparams(("parallel", "arbitrary", "arbitrary")),
        name="moba_decode",
    )(page_table, qn, kn, vn, *([cache_k] * DEC_PAGES), *([cache_v] * DEC_PAGES))


IN0_SPLITS = ((0, A_COLS), (A_COLS, B_WIDTH), (A_COLS + B_WIDTH, B_WIDTH), (A_COLS + 2 * B_WIDTH, B_WIDTH))


def _pad_tokens(a, t_pad):
    return jnp.pad(a, ((0, 0), (0, t_pad - a.shape[1]), (0, 0)))


def kernel(x_prompt, x_sample, cache_l0_k, cache_l0_v, state_l0_wkv, state_l0_shift, state_l1_conv, page_table, norm0_mix_g, w_in0, mu_shift, w_lora_up, w0, a_lora_up, a0, g_lora_up, k_k, k_a, r_k, lnx_g, lnx_b, q_norm_g, k_norm_g, w_out0, norm0_ffn_g, ffn_w_gate, ffn_w_up, ffn_w_down, norm1_mix_g, conv_w_in, conv_dw_w, conv_dw_b, conv_ln_g, conv_ln_b, conv_w_out, norm1_ffn_g, moe_router, moe_w_gate, moe_w_up, moe_w_down):
    bp, tp, d = x_prompt.shape
    bs, ts, _ = x_sample.shape
    n_p, n_s = bp * tp, bs * DEC_TOK
    cast = lambda a: a.astype(BF16)
    w_in0_b, w_out0_b = cast(w_in0), cast(w_out0)
    ffn_g_b, ffn_u_b, ffn_d_b = cast(ffn_w_gate), cast(ffn_w_up), cast(ffn_w_down)
    conv_in_b, conv_out_b = cast(conv_w_in), cast(conv_w_out)
    moe_g_b, moe_u_b, moe_d_b = cast(moe_w_gate), cast(moe_w_up), cast(moe_w_down)
    rk = r_k.reshape(-1)
    rwkv_w = (mu_shift, w_lora_up, w0, a_lora_up, a0, g_lora_up, k_k, k_a, rk)

    xp = x_prompt.reshape(n_p, d)
    xs = _pad_tokens(x_sample, DEC_TOK).reshape(n_s, d)

    p_a, q, k, v_p = norm_proj(xp, norm0_mix_g, w_in0_b, IN0_SPLITS, 512)
    p_a3 = p_a.reshape(bp, tp, A_COLS)
    r_, lw_, kt_, vv_, kap_, b_, g_, bonus_ = rwkv_prep(
        p_a3, jnp.zeros((bp, A_COLS), F32), *rwkv_w, 256)
    y_p, wkv_prompt = rwkv_chunked(r_, lw_, kt_, vv_, kap_, b_,
                                   jnp.zeros((bp, A_HEADS, HEAD_DIM, HEAD_DIM), F32), RWKV_CHUNK, tp, bp,
                                   RWKV_PASSES_PREFILL)
    kn_p, km, q_t, k_b, v_t = moba_prep_t(q, k, v_p, q_norm_g, k_norm_g, bp)
    o_p = moba_prefill(q_t, k_b.reshape(bp, tp, B_WIDTH), v_t, km.reshape(bp, tp // MOBA_BLOCK, B_WIDTH))
    flat = lambda a: a.reshape(-1, a.shape[-1])
    xp = mix_out(xp, flat(y_p), flat(bonus_), flat(g_), flat(o_p), lnx_g, lnx_b, w_out0_b, 512)
    xp = ffn(xp, norm0_ffn_g, ffn_g_b, ffn_u_b, ffn_d_b, 512, D_FF // 2)
    shift_prompt = p_a3[:, -1]

    ps_a, q, k, v_s = norm_proj(xs, norm0_mix_g, w_in0, IN0_SPLITS, n_s)
    ps_a3 = ps_a.reshape(bs, DEC_TOK, A_COLS)
    r_, lw_, kt_, vv_, kap_, b_, g_, bonus_ = rwkv_prep(ps_a3, state_l0_shift, *rwkv_w, DEC_TOK)
    y_s, wkv_sample = rwkv_chunked(r_, lw_, kt_, vv_, kap_, b_, state_l0_wkv, DEC_TOK, ts, 2,
                                   RWKV_PASSES_DECODE)
    qn, kn_s, _ = moba_prep(q, k, q_norm_g, k_norm_g, n_s)
    tok3 = lambda a: a.reshape(bs, DEC_TOK, B_WIDTH)
    pool = cache_l0_k.shape[0]
    o_s = moba_decode(tok3(qn), tok3(kn_s), tok3(v_s), cache_l0_k.reshape(pool, PAGE_SIZE, B_WIDTH),
                      cache_l0_v.reshape(pool, PAGE_SIZE, B_WIDTH), page_table, ts)
    xs = mix_out(xs, flat(y_s), flat(bonus_), flat(g_), flat(o_s), lnx_g, lnx_b, w_out0_b, n_s)
    xs = ffn(xs, norm0_ffn_g, ffn_g_b, ffn_u_b, ffn_d_b, n_s, D_FF // 2)
    shift_sample = ps_a3[:, ts - 1]

    u_p = conv_in(xp, norm1_mix_g, conv_in_b, 512).reshape(bp, tp, d)
    xp = conv_dw(u_p, jnp.zeros((bp, HALO, d), F32), xp.reshape(bp, tp, d), conv_dw_w, conv_dw_b,
                 conv_ln_g, conv_ln_b, conv_out_b, 256).reshape(n_p, d)
    xp = moe_layer(xp, norm1_ffn_g, moe_router, moe_g_b, moe_u_b, moe_d_b, MOE_ROWS)
    conv_prompt = u_p[:, tp - (CONV_K - 1):]

    u_s = conv_in(xs, norm1_mix_g, conv_in_b, n_s).reshape(bs, DEC_TOK, d)
    init = jnp.pad(state_l1_conv, ((0, 0), (HALO - (CONV_K - 1), 0), (0, 0)))
    xs = conv_dw(u_s, init, xs.reshape(bs, DEC_TOK, d), conv_dw_w, conv_dw_b,
                 conv_ln_g, conv_ln_b, conv_out_b, DEC_TOK).reshape(n_s, d)
    xs = moe_layer(xs, norm1_ffn_g, moe_router, moe_g_b, moe_u_b, moe_d_b, LANES)
    conv_sample = jnp.concatenate([state_l1_conv, u_s[:, :ts]], axis=1)[:, -(CONV_K - 1):]

    heads = lambda a, b_, t_: a.reshape(b_, -1, B_WIDTH)[:, :t_].reshape(b_, t_, B_HEADS, HEAD_DIM)
    return (xp.reshape(bp, tp, d), xs.reshape(bs, DEC_TOK, d)[:, :ts],
            heads(kn_p, bp, tp), heads(v_p, bp, tp), heads(kn_s, bs, ts), heads(v_s, bs, ts),
            wkv_prompt, wkv_sample, shift_prompt, shift_sample, conv_prompt, conv_sample)
```

```python
import functools

import jax
import jax.numpy as jnp
from jax import lax
from jax.experimental import pallas as pl
from jax.experimental.pallas import tpu as pltpu

F32 = jnp.float32
BF16 = jnp.bfloat16

D_MODEL = 1024
HEAD_DIM = 64
A_WIDTH = 512
A_HEADS = 8
B_WIDTH = 512
B_HEADS = 8
LORA_DECAY = 64
LORA_ICLR = 64
LORA_GATE = 128
A_COLS = 3 * A_WIDTH + LORA_DECAY + LORA_ICLR + LORA_GATE
MOBA_BLOCK = 256
MOBA_TOPK = 3
PAGE_SIZE = 128
ATTN_SCALE = HEAD_DIM ** -0.5
D_FF = 2816
N_EXPERTS = 8
D_FF_EXPERT = 3584
CONV_K = 31
RMS_EPS = 1e-6
GN_EPS = 64e-5
LN_EPS = 1e-5

LANES = 128
VMEM_LIMIT = 56 * 1024 * 1024
RWKV_CHUNK = 64
MOE_TF = 512
MOE_ROWS = 504
MOE_ROWS_DECODE = 112
HALO = 32
HI = lax.Precision.HIGHEST


def _cparams(sem):
    return pltpu.CompilerParams(dimension_semantics=sem, vmem_limit_bytes=VMEM_LIMIT)


def _rms(x, g):
    return x * lax.rsqrt(jnp.mean(x * x, axis=-1, keepdims=True) + RMS_EPS) * g


def _bdot(a, b):
    return jnp.dot(a.astype(BF16), b.astype(BF16), preferred_element_type=F32)


def _dot_nt(a, b, precision=None):
    return lax.dot_general(a, b, (((1,), (1,)), ((), ())), precision=precision,
                           preferred_element_type=F32)


def _seg_sum(x, bd):
    hi = x.astype(BF16)
    lo = (x - hi.astype(F32)).astype(BF16)
    return (jnp.dot(hi, bd, preferred_element_type=F32)
            + jnp.dot(lo, bd, preferred_element_type=F32))


def _block_diag_ones(n, seg):
    i = jnp.arange(n)
    return (i[:, None] // seg == i[None, :] // seg).astype(BF16)


def _norm_proj_kernel(x_ref, g_ref, w_ref, *out_refs, splits):
    h = _rms(x_ref[...], g_ref[...])
    if w_ref.dtype == F32:
        dot = functools.partial(jnp.dot, h, precision=HI, preferred_element_type=F32)
    else:
        dot = functools.partial(jnp.dot, h.astype(BF16), preferred_element_type=F32)
    for o_ref, (c0, cn) in zip(out_refs, splits):
        o_ref[...] = dot(w_ref[:, c0:c0 + cn])


def norm_proj(x, g, w_bf16, splits, tm):
    n, d = x.shape
    cols = w_bf16.shape[1]
    return pl.pallas_call(
        functools.partial(_norm_proj_kernel, splits=splits),
        grid=(n // tm,),
        in_specs=[pl.BlockSpec((tm, d), lambda i: (i, 0)),
                  pl.BlockSpec((1, d), lambda i: (0, 0)),
                  pl.BlockSpec((d, cols), lambda i: (0, 0))],
        out_specs=[pl.BlockSpec((tm, cn), lambda i: (i, 0)) for _, cn in splits],
        out_shape=[jax.ShapeDtypeStruct((n, cn), F32) for _, cn in splits],
        compiler_params=_cparams(("parallel",)),
        name="norm_proj",
    )(x, g.reshape(1, d), w_bf16)


def _rwkv_prep_kernel(p_ref, shift_ref, mu_ref, wl_ref, w0_ref, al_ref, a0_ref, gl_ref,
                      kk_ref, ka_ref, rk_ref, bd_ref,
                      r_o, lw_o, k_o, v_o, kap_o, b_o, g_o, bonus_o, carry):
    t = pl.program_id(1)

    @pl.when(t == 0)
    def _():
        carry[...] = shift_ref[0]

    p = p_ref[0]
    tt = p.shape[0]
    rolled = pltpu.roll(p, shift=1, axis=0)
    row = lax.broadcasted_iota(jnp.int32, p.shape, 0)
    prev = jnp.where(row == 0, carry[...], rolled)
    carry[...] = p[tt - 1:tt, :]
    xs = p + (prev - p) * mu_ref[...]
    aw = A_WIDTH
    r = xs[:, 0:aw]
    k = xs[:, aw:2 * aw]
    v = xs[:, 2 * aw:3 * aw]
    c0 = 3 * aw
    xw = xs[:, c0:c0 + LORA_DECAY]
    xa = xs[:, c0 + LORA_DECAY:c0 + LORA_DECAY + LORA_ICLR]
    xg = xs[:, c0 + LORA_DECAY + LORA_ICLR:]
    bd = bd_ref[...]

    z = w0_ref[...] + _bdot(jnp.tanh(xw), wl_ref[...])
    nz = -z
    softplus = jnp.maximum(nz, 0.0) + jnp.log(1.0 + jnp.exp(-jnp.abs(nz)))
    w_log = -softplus - 0.5
    lw_o[0] = -jnp.exp(w_log)
    a = jax.nn.sigmoid(a0_ref[...] + _bdot(xa, al_ref[...]))
    g_o[0] = _bdot(jax.nn.sigmoid(xg), gl_ref[...])
    kk = k * kk_ref[...]
    nrm = jnp.sqrt(_seg_sum(kk * kk, bd))
    kap = kk / jnp.maximum(nrm, 1e-12)
    k_h = k * (1.0 + (a - 1.0) * ka_ref[...])
    r_o[0] = r
    k_o[0] = k_h
    v_o[0] = v
    kap_o[0] = kap
    b_o[0] = kap * a
    bonus_o[0] = _seg_sum(r * k_h * rk_ref[...], bd) * v


def rwkv_prep(p_a, shift_prev, mu_shift, w_lora_up, w0, a_lora_up, a0, g_lora_up, k_k, k_a, r_k, tt):
    b, t, _ = p_a.shape
    row = lambda x: x.reshape(1, -1)
    const = lambda shape: pl.BlockSpec(shape, lambda i, j: (0,) * len(shape))
    tok = lambda w: pl.BlockSpec((1, tt, w), lambda i, j: (i, j, 0))
    outs = [jax.ShapeDtypeStruct((b, t, A_WIDTH), F32)] * 8
    return pl.pallas_call(
        _rwkv_prep_kernel,
        grid=(b, t // tt),
        in_specs=[tok(A_COLS),
                  pl.BlockSpec((1, 1, A_COLS), lambda i, j: (i, 0, 0)),
                  const((1, A_COLS)),
                  const((LORA_DECAY, A_WIDTH)), const((1, A_WIDTH)),
                  const((LORA_ICLR, A_WIDTH)), const((1, A_WIDTH)),
                  const((LORA_GATE, A_WIDTH)),
                  const((1, A_WIDTH)), const((1, A_WIDTH)), const((1, A_WIDTH)),
                  const((A_WIDTH, A_WIDTH))],
        out_specs=[tok(A_WIDTH)] * 8,
        out_shape=outs,
        scratch_shapes=[pltpu.VMEM((1, A_COLS), F32)],
        compiler_params=_cparams(("parallel", "arbitrary")),
        name="rwkv_prep",
    )(p_a, shift_prev.reshape(b, 1, A_COLS), row(mu_shift), w_lora_up.astype(BF16), row(w0),
      a_lora_up.astype(BF16), row(a0), g_lora_up.astype(BF16), row(k_k), row(k_a), row(r_k),
      _block_diag_ones(A_WIDTH, HEAD_DIM))


RWKV_PASSES_PREFILL = {"gram": 3, "inv": 1, "apply": 1, "state": 1}
RWKV_PASSES_DECODE = {"gram": 6, "inv": 6, "apply": 6, "state": 6}


def _split(x):
    hi = x.astype(BF16)
    return hi, (x - hi.astype(F32)).astype(BF16)


def _mm(a, b, passes, dims=(1, 0)):
    dn = (((dims[0],), (dims[1],)), ((), ()))
    dot = functools.partial(lax.dot_general, dimension_numbers=dn, preferred_element_type=F32)
    if passes == 6:
        return dot(a, b, precision=HI)
    if passes == 1:
        return dot(a.astype(BF16), b.astype(BF16))
    ah, al = _split(a)
    bh, bl = _split(b)
    return dot(ah, bh) + (dot(ah, bl) + dot(al, bh))


def _unit_lower_inverses(ls, n, c, passes):
    ri = lax.broadcasted_iota(jnp.int32, (n, n), 0)
    ci = lax.broadcasted_iota(jnp.int32, (n, n), 1)
    eye = (ri == ci).astype(F32)

    def same_block(bs):
        return (ri // bs) == (ci // bs)

    b0 = min(16, c)
    powers = [jnp.where(same_block(b0), -l, 0.0) for l in ls]
    invs = [eye + p for p in powers]
    span = 2
    while span < b0:
        powers = [_mm(p, p, passes) for p in powers]
        invs = [i + _mm(i, p, passes) for i, p in zip(invs, powers)]
        span *= 2
    bs = b0
    while bs < c:
        cross = same_block(2 * bs) & jnp.logical_not(same_block(bs))
        mids = [_mm(i, jnp.where(cross, l, 0.0), passes) for i, l in zip(invs, ls)]
        invs = [i - _mm(m, i, passes) for i, m in zip(invs, mids)]
        bs *= 2
    return invs


def _rwkv_chunk_kernel(r_ref, lw_ref, k_ref, v_ref, kap_ref, b_ref, s0_ref, y_ref, sout_ref, state,
                       *, chunk, t_valid, pw):
    ci = pl.program_id(1)
    c = chunk
    n = 2 * c
    hd = HEAD_DIM
    pairs = [(bi, j) for bi in range(r_ref.shape[0]) for j in range(A_HEADS // 2)]

    @pl.when(ci == 0)
    def _():
        z = jnp.zeros((hd, hd), F32)
        for idx, (bi, j) in enumerate(pairs):
            state[idx] = jnp.concatenate(
                [jnp.concatenate([s0_ref[bi, 2 * j], z], axis=1),
                 jnp.concatenate([z, s0_ref[bi, 2 * j + 1]], axis=1)], axis=0)

    lane = lax.broadcasted_iota(jnp.int32, (c, LANES), 1)
    trow = lax.broadcasted_iota(jnp.int32, (c, LANES), 0) + ci * c
    live = trow < t_valid
    m0 = lane < hd
    tri = (lax.broadcasted_iota(jnp.int32, (c, c), 0)
           >= lax.broadcasted_iota(jnp.int32, (c, c), 1)).astype(BF16)
    ri = lax.broadcasted_iota(jnp.int32, (n, n), 0)
    cj = lax.broadcasted_iota(jnp.int32, (n, n), 1)
    same = (ri // c) == (cj // c)
    strict = same & (ri > cj)
    incl = same & (ri >= cj)

    def stack(x):
        return jnp.concatenate([jnp.where(m0, x, 0.0), jnp.where(m0, 0.0, x)], axis=0)

    def cumulative(x):
        hi, lo = _split(x)
        lo2 = (x - hi.astype(F32) - lo.astype(F32)).astype(BF16)
        dot = functools.partial(jnp.dot, preferred_element_type=F32)
        return dot(tri, hi) + (dot(tri, lo) + dot(tri, lo2))

    each = lambda f, *cols: [f(*xs) for xs in zip(*cols)]
    sls = [(bi, slice(j * LANES, (j + 1) * LANES)) for bi, j in pairs]
    ld = lambda ref: [jnp.where(live, ref[bi, :, sl], 0.0) for bi, sl in sls]
    r = [r_ref[bi, :, sl] for bi, sl in sls]
    lw, k, v, kap, b = ld(lw_ref), ld(k_ref), ld(v_ref), ld(kap_ref), ld(b_ref)
    cum = each(cumulative, lw)
    total = [x[c - 1:c, :] for x in cum]
    ginv = [jnp.exp(-x) for x in cum]
    tail = each(lambda t, x: jnp.exp(t - x), total, cum)
    kap_s = each(lambda a, x, w: stack(a * jnp.exp(x - w)), kap, cum, lw)
    r_s = each(lambda a, x: stack(a * jnp.exp(x)), r, cum)
    b_s = each(lambda a, g: stack(a * g), b, ginv)
    k_s = each(lambda a, g: stack(a * g), k, ginv)
    v_s = each(stack, v)
    kbar_s = each(lambda a, t: stack(a * t), k, tail)
    bbar_s = each(lambda a, t: stack(a * t), b, tail)
    gram = lambda x, y: _mm(x, y, pw["gram"], (1, 1))
    l_b = each(lambda x, y: jnp.where(strict, gram(x, y), 0.0), kap_s, b_s)
    l_k = each(lambda x, y: jnp.where(strict, gram(x, y), 0.0), kap_s, k_s)
    a_qk = each(lambda x, y: jnp.where(incl, gram(x, y), 0.0), r_s, k_s)
    a_qb = each(lambda x, y: jnp.where(incl, gram(x, y), 0.0), r_s, b_s)
    inv = _unit_lower_inverses(l_b, n, c, pw["inv"])
    app = lambda x, y: _mm(x, y, pw["apply"])
    w_s = each(app, inv, kap_s)
    u0_s = each(app, inv, each(app, l_k, v_s))
    y0_s = each(app, a_qk, v_s)

    st = lambda x, y, dims=(1, 0): _mm(x, y, pw["state"], dims)
    s = [state[idx] for idx in range(len(pairs))]
    u_s = each(lambda w, s_, u0: st(w, s_, (1, 1)) + u0, w_s, s, u0_s)
    y_s = each(lambda r_, s_, y0, a, u: st(r_, s_, (1, 1)) + y0 - st(a, u), r_s, s, y0_s, a_qb, u_s)
    s_new = each(lambda s_, t, v_, kb, u, bb: s_ * jnp.exp(t) + st(v_, kb, (0, 0)) - st(u, bb, (0, 0)),
                 s, total, v_s, kbar_s, u_s, bbar_s)
    for idx, (bi, sl) in enumerate(sls):
        y_ref[bi, :, sl] = y_s[idx][0:c] + y_s[idx][c:n]
        state[idx] = s_new[idx]

    @pl.when(ci == pl.num_programs(1) - 1)
    def _():
        for idx, (bi, j) in enumerate(pairs):
            s = state[idx]
            sout_ref[bi, 2 * j] = s[0:hd, 0:hd]
            sout_ref[bi, 2 * j + 1] = s[hd:2 * hd, hd:2 * hd]


def rwkv_chunked(r, lw, k, v, kap, b, s0, chunk, t_valid, bb, passes):
    bsz, t, w = r.shape
    tok = pl.BlockSpec((bb, chunk, w), lambda i, c: (i, c, 0))
    st = pl.BlockSpec((bb, A_HEADS, HEAD_DIM, HEAD_DIM), lambda i, c: (i, 0, 0, 0))
    y, s_out = pl.pallas_call(
        functools.partial(_rwkv_chunk_kernel, chunk=chunk, t_valid=t_valid, pw=passes),
        grid=(bsz // bb, t // chunk),
        in_specs=[tok] * 6 + [st],
        out_specs=[tok, st],
        out_shape=[jax.ShapeDtypeStruct((bsz, t, w), F32),
                   jax.ShapeDtypeStruct((bsz, A_HEADS, HEAD_DIM, HEAD_DIM), F32)],
        scratch_shapes=[pltpu.VMEM((bb * A_HEADS // 2, LANES, LANES), F32)],
        compiler_params=_cparams(("parallel", "arbitrary")),
        name="rwkv_chunk",
    )(r, lw, k, v, kap, b, s0)
    return y, s_out


def _mix_out_kernel(x_ref, y_ref, bonus_ref, g_ref, o_ref, lng_ref, lnb_ref, bd_ref, w_ref, out_ref):
    y = y_ref[...]
    bd = bd_ref[...]
    mu = _seg_sum(y, bd) * (1.0 / HEAD_DIM)
    d = y - mu
    var = _seg_sum(d * d, bd) * (1.0 / HEAD_DIM)
    yn = d * lax.rsqrt(var + GN_EPS) * lng_ref[...] + lnb_ref[...]
    ya = (yn + bonus_ref[...]) * g_ref[...]
    acc = jnp.dot(ya.astype(BF16), w_ref[0:A_WIDTH, :], preferred_element_type=F32)
    acc += jnp.dot(o_ref[...].astype(BF16), w_ref[A_WIDTH:, :], preferred_element_type=F32)
    out_ref[...] = x_ref[...] + acc


def mix_out(x, y, bonus, g, o, lnx_g, lnx_b, w_out_bf16, tm):
    n, d = x.shape
    tok = lambda w: pl.BlockSpec((tm, w), lambda i: (i, 0))
    const = lambda shape: pl.BlockSpec(shape, lambda i: (0,) * len(shape))
    return pl.pallas_call(
        _mix_out_kernel,
        grid=(n // tm,),
        in_specs=[tok(d), tok(A_WIDTH), tok(A_WIDTH), tok(A_WIDTH), tok(B_WIDTH),
                  const((1, A_WIDTH)), const((1, A_WIDTH)), const((A_WIDTH, A_WIDTH)),
                  const((A_WIDTH + B_WIDTH, d))],
        out_specs=tok(d),
        out_shape=jax.ShapeDtypeStruct((n, d), F32),
        compiler_params=_cparams(("parallel",)),
        name="mix_out",
    )(x, y, bonus, g, o, lnx_g.reshape(1, -1), lnx_b.reshape(1, -1),
      _block_diag_ones(A_WIDTH, HEAD_DIM), w_out_bf16)


def _ffn_kernel(x_ref, g_ref, wg_ref, wu_ref, wd_ref, out_ref, h_scr, acc):
    j = pl.program_id(1)

    @pl.when(j == 0)
    def _():
        h_scr[...] = _rms(x_ref[...], g_ref[...]).astype(BF16)
        acc[...] = jnp.zeros_like(acc)

    h = h_scr[...]
    gate = jnp.dot(h, wg_ref[...], preferred_element_type=F32)
    up = jnp.dot(h, wu_ref[...], preferred_element_type=F32)
    act = (jax.nn.silu(gate) * up).astype(BF16)
    acc[...] += jnp.dot(act, wd_ref[...], preferred_element_type=F32)

    @pl.when(j == pl.num_programs(1) - 1)
    def _():
        out_ref[...] = x_ref[...] + acc[...]


def ffn(x, g, wg, wu, wd, tm, tf):
    n, d = x.shape
    f = wg.shape[1]
    return pl.pallas_call(
        _ffn_kernel,
        grid=(n // tm, f // tf),
        in_specs=[pl.BlockSpec((tm, d), lambda i, j: (i, 0)),
                  pl.BlockSpec((1, d), lambda i, j: (0, 0)),
                  pl.BlockSpec((d, tf), lambda i, j: (0, j)),
                  pl.BlockSpec((d, tf), lambda i, j: (0, j)),
                  pl.BlockSpec((tf, d), lambda i, j: (j, 0))],
        out_specs=pl.BlockSpec((tm, d), lambda i, j: (i, 0)),
        out_shape=jax.ShapeDtypeStruct((n, d), F32),
        scratch_shapes=[pltpu.VMEM((tm, d), BF16), pltpu.VMEM((tm, d), F32)],
        compiler_params=_cparams(("parallel", "arbitrary")),
        name="ffn",
    )(x, g.reshape(1, d), wg, wu, wd)


def _conv_in_kernel(x_ref, g_ref, w_ref, u_ref):
    h = _rms(x_ref[...], g_ref[...]).astype(BF16)
    d = u_ref.shape[1]
    a = jnp.dot(h, w_ref[:, 0:d], preferred_element_type=F32)
    gate = jnp.dot(h, w_ref[:, d:2 * d], preferred_element_type=F32)
    u_ref[...] = a * jax.nn.sigmoid(gate)


def conv_in(x, g, w_bf16, tm):
    n, d = x.shape
    return pl.pallas_call(
        _conv_in_kernel,
        grid=(n // tm,),
        in_specs=[pl.BlockSpec((tm, d), lambda i: (i, 0)),
                  pl.BlockSpec((1, d), lambda i: (0, 0)),
                  pl.BlockSpec((d, 2 * d), lambda i: (0, 0))],
        out_specs=pl.BlockSpec((tm, d), lambda i: (i, 0)),
        out_shape=jax.ShapeDtypeStruct((n, d), F32),
        compiler_params=_cparams(("parallel",)),
        name="conv_in",
    )(x, g.reshape(1, d), w_bf16)


def _conv_dw_kernel(u_ref, init_ref, x_ref, dw_ref, db_ref, lg_ref, lb_ref, w_ref, out_ref, ctx):
    t = pl.program_id(1)
    tt = u_ref.shape[1]

    @pl.when(t == 0)
    def _():
        ctx[0:HALO, :] = init_ref[0]

    ctx[HALO:HALO + tt, :] = u_ref[0]
    acc = jnp.zeros((tt, D_MODEL), F32) + db_ref[...]
    for j in range(CONV_K):
        acc = acc + ctx[pl.ds(HALO - (CONV_K - 1) + j, tt), :] * dw_ref[j:j + 1, :]
    keep = ctx[tt:tt + HALO, :]
    ctx[0:HALO, :] = keep
    mu = jnp.mean(acc, axis=-1, keepdims=True)
    d = acc - mu
    var = jnp.mean(d * d, axis=-1, keepdims=True)
    cf = d * lax.rsqrt(var + LN_EPS) * lg_ref[...] + lb_ref[...]
    out_ref[0] = x_ref[0] + jnp.dot(jax.nn.silu(cf).astype(BF16), w_ref[...], preferred_element_type=F32)


def conv_dw(u, init, x, dw_w, dw_b, ln_g, ln_b, w_out_bf16, tt):
    b, t, d = u.shape
    dw = jnp.pad(dw_w, ((0, HALO - CONV_K), (0, 0)))
    tok = pl.BlockSpec((1, tt, d), lambda i, j: (i, j, 0))
    const = lambda shape: pl.BlockSpec(shape, lambda i, j: (0,) * len(shape))
    return pl.pallas_call(
        _conv_dw_kernel,
        grid=(b, t // tt),
        in_specs=[tok, pl.BlockSpec((1, HALO, d), lambda i, j: (i, 0, 0)), tok,
                  const((HALO, d)), const((1, d)), const((1, d)), const((1, d)), const((d, d))],
        out_specs=tok,
        out_shape=jax.ShapeDtypeStruct((b, t, d), F32),
        scratch_shapes=[pltpu.VMEM((HALO + tt, d), F32)],
        compiler_params=_cparams(("parallel", "arbitrary")),
        name="conv_dw",
    )(u, init, x, dw, dw_b.reshape(1, d), ln_g.reshape(1, d), ln_b.reshape(1, d), w_out_bf16)


def _router_kernel(x_ref, g_ref, wr_ref, idx_ref, pw_ref):
    h = _rms(x_ref[...], g_ref[...])
    logits = _bdot(h, wr_ref[...])
    lane = lax.broadcasted_iota(jnp.int32, logits.shape, 1)
    valid = lane < N_EXPERTS
    logits = jnp.where(valid, logits, -jnp.inf)
    e = jnp.exp(logits - jnp.max(logits, axis=-1, keepdims=True))
    probs = jnp.where(valid, e / jnp.sum(e, axis=-1, keepdims=True), -1.0)
    m1 = jnp.max(probs, axis=-1, keepdims=True)
    i1 = jnp.min(jnp.where(probs == m1, lane, LANES), axis=-1, keepdims=True)
    rest = jnp.where(lane == i1, -1.0, probs)
    m2 = jnp.max(rest, axis=-1, keepdims=True)
    i2 = jnp.min(jnp.where(rest == m2, lane, LANES), axis=-1, keepdims=True)
    den = m1 + m2
    idx_ref[...] = jnp.where(lane == 0, i1, jnp.where(lane == 1, i2, 0))
    pw_ref[...] = jnp.where(lane == 0, m1 / den, jnp.where(lane == 1, m2 / den, 0.0))


def router(x, g, w_router, tm):
    n, d = x.shape
    wr = jnp.pad(w_router, ((0, 0), (0, LANES - N_EXPERTS))).astype(BF16)
    return pl.pallas_call(
        _router_kernel,
        grid=(n // tm,),
        in_specs=[pl.BlockSpec((tm, d), lambda i: (i, 0)),
                  pl.BlockSpec((1, d), lambda i: (0, 0)),
                  pl.BlockSpec((d, LANES), lambda i: (0, 0))],
        out_specs=[pl.BlockSpec((tm, LANES), lambda i: (i, 0))] * 2,
        out_shape=[jax.ShapeDtypeStruct((n, LANES), jnp.int32), jax.ShapeDtypeStruct((n, LANES), F32)],
        compiler_params=_cparams(("parallel",)),
        name="moe_router",
    )(x, g.reshape(1, d), wr)


def _moe_kernel(be_ref, nu_ref, tok_ref, dst_ref,
                x_hbm, g_ref, p_ref, wg_ref, wu_ref, wd_ref,
                y_hbm, xg, yst, h_scr, acc, sem_in, sem_out, *, nj):
    i = pl.program_id(0)
    j = pl.program_id(1)
    m = xg.shape[1]
    part = m // nj
    n_used = nu_ref[0]
    valid = i < n_used
    cur = i % 2
    oth = (i + 1) % 2
    last_j = j == nj - 1

    def row_in(slot, r, tok):
        return pltpu.make_async_copy(x_hbm.at[pl.ds(tok, 1), :], xg.at[slot, pl.ds(r, 1), :], sem_in.at[slot])

    def row_out(slot, r, dst):
        return pltpu.make_async_copy(yst.at[slot, pl.ds(r, 1), :], y_hbm.at[pl.ds(dst, 1), :], sem_out.at[slot])

    def all_in(slot):
        return pltpu.make_async_copy(x_hbm.at[pl.ds(0, m), :], xg.at[slot], sem_in.at[slot])

    def all_out(slot):
        return pltpu.make_async_copy(yst.at[slot], y_hbm.at[pl.ds(0, m), :], sem_out.at[slot])

    def start_scatter_part():
        for t in range(part):
            r = j * part + t
            row_out(oth, r, dst_ref[(i - 1) * m + r]).start()

    @pl.when((i == 0) & (j == 0))
    def _():
        yst[1] = jnp.zeros((m, yst.shape[2]), F32)
        n_real = y_hbm.shape[0] - 2 * m
        spare = [pltpu.make_async_copy(yst.at[1], y_hbm.at[pl.ds(n_real + k * m, m), :], sem_out.at[1])
                 for k in range(2)]
        for cp in spare:
            cp.start()
        for cp in spare:
            cp.wait()

    @pl.when((i == 0) & (j == 0) & valid)
    def _():
        def start(r, c):
            row_in(0, r, tok_ref[r]).start()
            return c
        lax.fori_loop(0, m, start, 0)

    @pl.when((j == 0) & (i <= n_used) & (n_used > 0))
    def _():
        all_in(cur).wait()

    def compute(scatter_previous):
        @pl.when(j == 0)
        def _():
            h_scr[...] = _rms(xg[cur], g_ref[...]).astype(BF16)
            acc[...] = jnp.zeros_like(acc)

        for t in range(part):
            r = j * part + t
            row_in(oth, r, tok_ref[(i + 1) * m + r]).start()
        if scatter_previous:
            start_scatter_part()
        h = h_scr[...]
        gate = jnp.dot(h, wg_ref[0, 0], preferred_element_type=F32)
        up = jnp.dot(h, wu_ref[0, 0], preferred_element_type=F32)
        act = (jax.nn.silu(gate) * up).astype(BF16)
        acc[...] += jnp.dot(act, wd_ref[0, 0], preferred_element_type=F32)

        @pl.when(last_j)
        def _():
            @pl.when(i >= 2)
            def _():
                all_out(cur).wait()
            yst[cur] = acc[...] * p_ref[...]

    @pl.when(valid & (i == 0))
    def _():
        compute(False)

    @pl.when(valid & (i > 0))
    def _():
        compute(True)

    @pl.when((i == n_used) & (i > 0))
    def _():
        start_scatter_part()

    @pl.when(last_j & (i >= 2) & (i >= n_used) & (i - 2 < n_used))
    def _():
        all_out(cur).wait()


def tile_expert_weights(w_gate, w_up, w_down, tf):
    e, d, f = w_gate.shape
    cols = lambda w: w.astype(BF16).reshape(e, d, f // tf, tf).transpose(0, 2, 1, 3)
    return cols(w_gate), cols(w_up), w_down.astype(BF16).reshape(e, f // tf, tf, d)


def moe_experts(x, g, block_e, n_used, buf_tok, buf_dst, buf_p, wg, wu, wd, m):
    n, d = x.shape
    n_blocks = block_e.shape[0]
    nj, tf = wg.shape[1], wg.shape[3]
    assert m % nj == 0 and m % 8 == 0

    def jj(i, j, nu):
        return jnp.where(i < nu[0], j, nj - 1)

    grid_spec = pltpu.PrefetchScalarGridSpec(
        num_scalar_prefetch=4,
        grid=(n_blocks, nj),
        in_specs=[pl.BlockSpec(memory_space=pl.ANY),
                  pl.BlockSpec((1, d), lambda i, j, be, nu, tk, ds: (0, 0)),
                  pl.BlockSpec((m, 1), lambda i, j, be, nu, tk, ds: (i, 0)),
                  pl.BlockSpec((1, 1, d, tf), lambda i, j, be, nu, tk, ds: (be[i], jj(i, j, nu), 0, 0)),
                  pl.BlockSpec((1, 1, d, tf), lambda i, j, be, nu, tk, ds: (be[i], jj(i, j, nu), 0, 0)),
                  pl.BlockSpec((1, 1, tf, d), lambda i, j, be, nu, tk, ds: (be[i], jj(i, j, nu), 0, 0))],
        out_specs=pl.BlockSpec(memory_space=pl.ANY),
        scratch_shapes=[pltpu.VMEM((2, m, d), F32), pltpu.VMEM((2, m, d), F32), pltpu.VMEM((m, d), BF16),
                        pltpu.VMEM((m, d), F32), pltpu.SemaphoreType.DMA((2,)), pltpu.SemaphoreType.DMA((2,))],
    )
    return pl.pallas_call(
        functools.partial(_moe_kernel, nj=nj),
        grid_spec=grid_spec,
        out_shape=jax.ShapeDtypeStruct((2 * n + 2 * m, d), F32),
        compiler_params=_cparams(("arbitrary", "arbitrary")),
        name="moe_experts",
    )(block_e, n_used, buf_tok, buf_dst, x, g.reshape(1, d), buf_p, wg, wu, wd)


def _combine_kernel(x_ref, a_ref, b_ref, o_ref):
    o_ref[...] = x_ref[...] + (a_ref[...] + b_ref[...])


def moe_combine(x, y2, tm):
    n, d = x.shape
    half = n // tm
    return pl.pallas_call(
        _combine_kernel,
        grid=(half,),
        in_specs=[pl.BlockSpec((tm, d), lambda i: (i, 0)),
                  pl.BlockSpec((tm, d), lambda i: (i, 0)),
                  pl.BlockSpec((tm, d), lambda i: (i + half, 0))],
        out_specs=pl.BlockSpec((tm, d), lambda i: (i, 0)),
        out_shape=jax.ShapeDtypeStruct((n, d), F32),
        compiler_params=_cparams(("parallel",)),
        name="moe_combine",
    )(x, y2, y2)


def moe_layer(x, g, w_router, wg, wu, wd, m):
    n, d = x.shape
    idx, pw = router(x, g, w_router, min(n, 512))
    flat_e = idx[:, :2].reshape(-1)
    flat_p = pw[:, :2].reshape(-1)
    onehot = (flat_e[:, None] == jnp.arange(N_EXPERTS, dtype=jnp.int32)[None, :]).astype(jnp.int32)
    rank = jnp.sum((jnp.cumsum(onehot, axis=0) - onehot) * onehot, axis=1)
    counts = jnp.sum(onehot, axis=0)
    padded = (counts + m - 1) // m * m
    pend = jnp.cumsum(padded)
    slot = (pend - padded)[flat_e] + rank
    n_blocks = (2 * n) // m + N_EXPERTS + 2
    n_slots = n_blocks * m
    owner = jnp.zeros((n_slots,), jnp.int32).at[slot].set(jnp.arange(1, 2 * n + 1, dtype=jnp.int32))
    pair = jnp.maximum(owner - 1, 0)
    buf_tok = pair // 2
    s_idx = jnp.arange(n_slots, dtype=jnp.int32)
    spare = 2 * n + (s_idx // m % 2) * m + s_idx % m
    buf_dst = jnp.where(owner > 0, (pair % 2) * n + pair // 2, spare)
    buf_p = jnp.where(owner > 0, flat_p[pair], 0.0).reshape(n_slots, 1)
    block_e = jnp.minimum(jnp.searchsorted(pend, jnp.arange(n_blocks, dtype=jnp.int32) * m, side='right'),
                          N_EXPERTS - 1).astype(jnp.int32)
    n_used = (pend[-1] // m).astype(jnp.int32).reshape(1)
    y2 = moe_experts(x, g, block_e, n_used, buf_tok, buf_dst, buf_p, wg, wu, wd, m)
    return moe_combine(x, y2, min(n, 512))


def _moba_prep_kernel(q_ref, k_ref, qg_ref, kg_ref, bd_ref, qn_ref, kn_ref, km_ref):
    bd = bd_ref[...]

    def head_norm(x, g):
        ms = _seg_sum(x * x, bd) * (1.0 / HEAD_DIM)
        return x * lax.rsqrt(ms + RMS_EPS) * g

    qn_ref[...] = head_norm(q_ref[...], qg_ref[...])
    kn = head_norm(k_ref[...], kg_ref[...])
    kn_ref[...] = kn
    km_ref[0] = jnp.mean(kn, axis=0, keepdims=True)


def moba_prep(q, k, q_norm_g, k_norm_g, tm):
    n, w = q.shape
    tok = pl.BlockSpec((tm, w), lambda i: (i, 0))
    const = lambda shape: pl.BlockSpec(shape, lambda i: (0,) * len(shape))
    return pl.pallas_call(
        _moba_prep_kernel,
        grid=(n // tm,),
        in_specs=[tok, tok, const((1, w)), const((1, w)), const((w, w))],
        out_specs=[tok, tok, pl.BlockSpec((1, 1, w), lambda i: (i, 0, 0))],
        out_shape=[jax.ShapeDtypeStruct((n, w), F32), jax.ShapeDtypeStruct((n, w), F32),
                   jax.ShapeDtypeStruct((n // tm, 1, w), F32)],
        compiler_params=_cparams(("parallel",)),
        name="moba_prep",
    )(q, k, jnp.tile(q_norm_g, B_HEADS).reshape(1, w), jnp.tile(k_norm_g, B_HEADS).reshape(1, w),
      _block_diag_ones(w, HEAD_DIM))


def _top_blocks(gate, bidx, nb):
    chosen = jnp.zeros(gate.shape, F32)
    for _ in range(MOBA_TOPK):
        mx = jnp.max(gate, axis=-1, keepdims=True)
        first = jnp.min(jnp.where(gate == mx, bidx, nb), axis=-1, keepdims=True)
        chosen = jnp.where((bidx == first) & (mx > -jnp.inf), 1.0, chosen)
        gate = jnp.where(bidx == first, -jnp.inf, gate)
    return chosen


def _moba_prep_t_kernel(q_ref, k_ref, v_ref, qg_ref, kg_ref, bd_ref, kn_ref, km_ref, qt_ref, kb_ref, vt_ref):
    bd = bd_ref[...]

    def head_norm(x, g):
        ms = _seg_sum(x * x, bd) * (1.0 / HEAD_DIM)
        return x * lax.rsqrt(ms + RMS_EPS) * g

    qn = head_norm(q_ref[...], qg_ref[...])
    kn = head_norm(k_ref[...], kg_ref[...])
    kn_ref[...] = kn
    kb_ref[...] = kn.astype(BF16)
    km_ref[0] = jnp.mean(kn, axis=0, keepdims=True)
    qt_ref[0] = (qn * ATTN_SCALE).T.astype(BF16)
    vt_ref[0] = v_ref[...].T.astype(BF16)


def moba_prep_t(q, k, v, q_norm_g, k_norm_g, bsz):
    n, w = q.shape
    tm = MOBA_BLOCK
    nblk = n // bsz // tm
    tok = pl.BlockSpec((tm, w), lambda i: (i, 0))
    feat = pl.BlockSpec((1, w, tm), lambda i: (i // nblk, 0, i % nblk))
    const = lambda shape: pl.BlockSpec(shape, lambda i: (0,) * len(shape))
    return pl.pallas_call(
        _moba_prep_t_kernel,
        grid=(n // tm,),
        in_specs=[tok, tok, tok, const((1, w)), const((1, w)), const((w, w))],
        out_specs=[tok, pl.BlockSpec((1, 1, w), lambda i: (i, 0, 0)), feat, tok, feat],
        out_shape=[jax.ShapeDtypeStruct((n, w), F32), jax.ShapeDtypeStruct((n // tm, 1, w), F32),
                   jax.ShapeDtypeStruct((bsz, w, n // bsz), BF16), jax.ShapeDtypeStruct((n, w), BF16),
                   jax.ShapeDtypeStruct((bsz, w, n // bsz), BF16)],
        compiler_params=_cparams(("parallel",)),
        name="moba_prep_t",
    )(q, k, v, jnp.tile(q_norm_g, B_HEADS).reshape(1, w), jnp.tile(k_norm_g, B_HEADS).reshape(1, w),
      _block_diag_ones(w, HEAD_DIM))


def _moba_attn_kernel(qi_ref, ki_ref, qt_ref, k_ref, vt_ref, km_ref, o_ref,
                      qz, sel, bias, s_scr, m_scr, l_scr, acc):
    s = pl.program_id(1)
    qi = qi_ref[s]
    ki = ki_ref[s]
    blk = MOBA_BLOCK
    nb = km_ref.shape[1]
    hd = HEAD_DIM
    is_own = ki == qi
    key_i = lax.broadcasted_iota(jnp.int32, (blk, blk), 0)
    qry_i = lax.broadcasted_iota(jnp.int32, (blk, blk), 1)
    rel = key_i - qry_i
    blk_i = lax.broadcasted_iota(jnp.int32, (nb, blk), 0)
    step_off = jnp.full((1, 1), (qi - ki) * blk, jnp.int32).astype(F32)

    slopes = [2.0 ** (-8.0 * (h + 1) / B_HEADS) for h in range(B_HEADS)]

    def scores(h, own):
        hp = h // 2
        raw = jnp.dot(k_ref[0, :, hp * LANES:(hp + 1) * LANES], qz[h], preferred_element_type=F32)
        sp = raw + bias[h]
        if own:
            sp = jnp.where(rel <= 0, sp, -jnp.inf)
        s_scr[h] = sp
        return jnp.max(sp, axis=0, keepdims=True)

    def attend(h, own, top):
        if own:
            m_new = top
            p = jnp.exp(s_scr[h] - m_new)
            l_scr[h] = jnp.sum(p, axis=0, keepdims=True)
        else:
            live = sel[h, pl.ds(ki, 1), :] > 0.5
            off = slopes[h] * step_off
            m_prev = m_scr[h]
            m_new = jnp.maximum(m_prev, jnp.where(live, top - off, -jnp.inf))
            alpha = jnp.exp(m_prev - m_new)
            p = jnp.exp(s_scr[h] - jnp.where(live, m_new + off, jnp.inf))
            l_scr[h] = alpha * l_scr[h] + jnp.sum(p, axis=0, keepdims=True)
        pv = jnp.dot(vt_ref[0, h * hd:(h + 1) * hd, :], p.astype(BF16), preferred_element_type=F32)
        acc[h] = pv if own else alpha * acc[h] + pv
        m_scr[h] = m_new

    def all_heads(own):
        tops = [scores(h, own) for h in range(B_HEADS)]
        for h in range(B_HEADS):
            attend(h, own, tops[h])

    @pl.when(is_own)
    def _():
        frow = lax.broadcasted_iota(jnp.int32, (LANES, blk), 0)
        for h in range(B_HEADS):
            hp = h // 2
            qt = qt_ref[0, hp * LANES:(hp + 1) * LANES, :]
            mine = (frow < hd) if h % 2 == 0 else (frow >= hd)
            qh = jnp.where(mine, qt, jnp.zeros_like(qt))
            qz[h] = qh
            gate = jnp.dot(km_ref[0, :, hp * LANES:(hp + 1) * LANES].astype(BF16), qh,
                           preferred_element_type=F32)
            gate = jnp.where(blk_i < qi, gate, -jnp.inf)
            chosen = jnp.zeros((nb, blk), F32)
            for _ in range(MOBA_TOPK):
                mx = jnp.max(gate, axis=0, keepdims=True)
                first = jnp.min(jnp.where(gate == mx, blk_i, nb), axis=0, keepdims=True)
                chosen = jnp.where((blk_i == first) & (mx > -jnp.inf), 1.0, chosen)
                gate = jnp.where(blk_i == first, -jnp.inf, gate)
            sel[h] = chosen
            bias[h] = rel.astype(F32) * slopes[h]
        all_heads(True)

    @pl.when(jnp.logical_not(is_own))
    def _():
        all_heads(False)

    @pl.when((ki == qi - 1) | (qi == 0))
    def _():
        for hp in range(B_HEADS // 2):
            pair = jnp.concatenate([acc[2 * hp] / l_scr[2 * hp], acc[2 * hp + 1] / l_scr[2 * hp + 1]], axis=0)
            o_ref[0, :, hp * LANES:(hp + 1) * LANES] = pair.T


def moba_prefill(qt, kb, vt, km):
    b, w, t = qt.shape
    blk = MOBA_BLOCK
    nb = t // blk
    pairs = [(qi, ki) for qi in range(nb) for ki in [qi] + list(range(qi))]
    qi_tab = jnp.array([p[0] for p in pairs], jnp.int32)
    ki_tab = jnp.array([p[1] for p in pairs], jnp.int32)
    grid_spec = pltpu.PrefetchScalarGridSpec(
        num_scalar_prefetch=2,
        grid=(b, len(pairs)),
        in_specs=[pl.BlockSpec((1, w, blk), lambda i, s, qt_, kt_: (i, 0, qt_[s])),
                  pl.BlockSpec((1, blk, w), lambda i, s, qt_, kt_: (i, kt_[s], 0)),
                  pl.BlockSpec((1, w, blk), lambda i, s, qt_, kt_: (i, 0, kt_[s])),
                  pl.BlockSpec((1, nb, w), lambda i, s, qt_, kt_: (i, 0, 0))],
        out_specs=pl.BlockSpec((1, blk, w), lambda i, s, qt_, kt_: (i, qt_[s], 0)),
        scratch_shapes=[pltpu.VMEM((B_HEADS, LANES, blk), BF16), pltpu.VMEM((B_HEADS, nb, blk), F32),
                        pltpu.VMEM((B_HEADS, blk, blk), F32), pltpu.VMEM((B_HEADS, blk, blk), F32),
                        pltpu.VMEM((B_HEADS, 1, blk), F32),
                        pltpu.VMEM((B_HEADS, 1, blk), F32), pltpu.VMEM((B_HEADS, HEAD_DIM, blk), F32)],
    )
    return pl.pallas_call(
        _moba_attn_kernel,
        grid_spec=grid_spec,
        out_shape=jax.ShapeDtypeStruct((b, t, w), F32),
        compiler_params=_cparams(("parallel", "arbitrary")),
        name="moba_prefill",
    )(qi_tab, ki_tab, qt, kb, vt, km)


DEC_TOK = 8
DEC_PAGES = 8


def _moba_dec_kernel(pt_ref, q_ref, kn_ref, vn_ref, *refs, t_new, past_len):
    npg = DEC_PAGES
    k_refs, v_refs = refs[:npg], refs[npg:2 * npg]
    o_ref, kc, km_scr, sel, s_scr, m_scr, l_scr, acc = refs[2 * npg:]
    ph = pl.program_id(1)
    g = pl.program_id(2)
    last_g = g == pl.num_programs(2) - 1
    nh = B_HEADS
    rows = nh * DEC_TOK
    w = B_WIDTH
    per_blk = MOBA_BLOCK // PAGE_SIZE
    nb = km_scr.shape[0]

    qt = jnp.concatenate([q_ref[0]] * nh, axis=0)
    own_lanes = (lax.broadcasted_iota(jnp.int32, (rows, w), 1) // HEAD_DIM
                 == lax.broadcasted_iota(jnp.int32, (rows, w), 0) // DEC_TOK)
    qf = jnp.where(own_lanes, qt, 0.0)
    qb = (qf * ATTN_SCALE).astype(BF16)
    rcol = lax.broadcasted_iota(jnp.int32, (rows, 1), 0)
    tok = rcol % DEC_TOK
    slope = jnp.exp2(-8.0 * (rcol // DEC_TOK + 1).astype(F32) / nh)
    bidx = lax.broadcasted_iota(jnp.int32, (rows, nb), 1)

    @pl.when(ph == 0)
    def _():
        for i in range(0, npg, per_blk):
            tot = jnp.zeros((1, w), F32)
            for u in range(per_blk):
                kp = k_refs[i + u][0]
                kc[g * npg + i + u] = kp.astype(BF16)
                tot = tot + jnp.sum(kp, axis=0, keepdims=True)
            km_scr[pl.ds(g * (npg // per_blk) + i // per_blk, 1), :] = tot * (1.0 / MOBA_BLOCK)

        @pl.when(last_g)
        def _():
            gate = _dot_nt(qf, km_scr[...], HI)
            sel[...] = _top_blocks(gate, bidx, nb)
            c = lax.broadcasted_iota(jnp.int32, (rows, DEC_TOK), 1)
            sc = _dot_nt(qb, kn_ref[0].astype(BF16)) - slope * (tok - c).astype(F32)
            sc = jnp.where((c <= tok) & (c < t_new), sc, -jnp.inf)
            m_new = jnp.max(sc, axis=-1, keepdims=True)
            p = jnp.exp(sc - m_new)
            m_scr[...] = m_new
            l_scr[...] = jnp.sum(p, axis=-1, keepdims=True)
            acc[...] = jnp.dot(p.astype(BF16), vn_ref[0].astype(BF16), preferred_element_type=F32)

    @pl.when(ph == 1)
    def _():
        key_bias = slope * lax.broadcasted_iota(jnp.int32, (rows, PAGE_SIZE), 1).astype(F32)
        tops = []
        for i in range(npg):
            page = g * npg + i
            picked = jnp.sum(jnp.where(bidx == page // per_blk, sel[...], 0.0),
                             axis=-1, keepdims=True) > 0.5
            dist0 = (past_len + tok - page * PAGE_SIZE).astype(F32)
            sc = _dot_nt(qb, kc[page]) + key_bias + jnp.where(picked, -slope * dist0, -jnp.inf)
            s_scr[i] = sc
            tops.append(jnp.max(sc, axis=-1, keepdims=True))
        m_prev = m_scr[...]
        m_new = functools.reduce(jnp.maximum, tops, m_prev)
        alpha = jnp.exp(m_prev - m_new)
        lsum = jnp.zeros((rows, 1), F32)
        pv = jnp.zeros((rows, w), F32)
        for i in range(npg):
            p = jnp.exp(s_scr[i] - m_new)
            lsum = lsum + jnp.sum(p, axis=-1, keepdims=True)
            pv = pv + jnp.dot(p.astype(BF16), v_refs[i][0].astype(BF16), preferred_element_type=F32)
        l_scr[...] = alpha * l_scr[...] + lsum
        acc[...] = alpha * acc[...] + pv
        m_scr[...] = m_new

        @pl.when(last_g)
        def _():
            o = jnp.where(own_lanes, acc[...] / l_scr[...], 0.0)
            o_ref[0] = jnp.sum(o.reshape(nh, DEC_TOK, w), axis=0)


def moba_decode(qn, kn, vn, cache_k, cache_v, page_table, t_new):
    s, _, w = qn.shape
    n_pages = page_table.shape[1]
    past_len = n_pages * PAGE_SIZE
    assert past_len % MOBA_BLOCK == 0 and n_pages % DEC_PAGES == 0 and t_new <= DEC_TOK
    nb = past_len // MOBA_BLOCK
    rows = B_HEADS * DEC_TOK
    new = pl.BlockSpec((1, DEC_TOK, w), lambda b, ph, g, pt: (b, 0, 0))
    last = n_pages // DEC_PAGES - 1

    def k_spec(i):
        return pl.BlockSpec((1, PAGE_SIZE, w),
                            lambda b, ph, g, pt: (pt[b, jnp.where(ph == 0, g, last) * DEC_PAGES + i], 0, 0))

    def v_spec(i):
        return pl.BlockSpec((1, PAGE_SIZE, w),
                            lambda b, ph, g, pt: (pt[b, jnp.where(ph == 0, 0, g) * DEC_PAGES + i], 0, 0))

    grid_spec = pltpu.PrefetchScalarGridSpec(
        num_scalar_prefetch=1,
        grid=(s, 2, n_pages // DEC_PAGES),
        in_specs=[new, new, new] + [k_spec(i) for i in range(DEC_PAGES)] + [v_spec(i) for i in range(DEC_PAGES)],
        out_specs=new,
        scratch_shapes=[pltpu.VMEM((n_pages, PAGE_SIZE, w), BF16), pltpu.VMEM((nb, w), F32),
                        pltpu.VMEM((rows, nb), F32), pltpu.VMEM((DEC_PAGES, rows, PAGE_SIZE), F32),
                        pltpu.VMEM((rows, 1), F32), pltpu.VMEM((rows, 1), F32), pltpu.VMEM((rows, w), F32)],
    )
    return pl.pallas_call(
        functools.partial(_moba_dec_kernel, t_new=t_new, past_len=past_len),
        grid_spec=grid_spec,
        out_shape=jax.ShapeDtypeStruct((s, DEC_TOK, w), F32),
        compiler_params=_cparams(("parallel", "arbitrary", "arbitrary")),
        name="moba_decode",
    )(page_table, qn, kn, vn, *([cache_k] * DEC_PAGES), *([cache_v] * DEC_PAGES))


IN0_SPLITS = ((0, A_COLS), (A_COLS, B_WIDTH), (A_COLS + B_WIDTH, B_WIDTH), (A_COLS + 2 * B_WIDTH, B_WIDTH))


def _pad_tokens(a, t_pad):
    return jnp.pad(a, ((0, 0), (0, t_pad - a.shape[1]), (0, 0)))


def kernel(x_prompt, x_sample, cache_l0_k, cache_l0_v, state_l0_wkv, state_l0_shift, state_l1_conv, page_table, norm0_mix_g, w_in0, mu_shift, w_lora_up, w0, a_lora_up, a0, g_lora_up, k_k, k_a, r_k, lnx_g, lnx_b, q_norm_g, k_norm_g, w_out0, norm0_ffn_g, ffn_w_gate, ffn_w_up, ffn_w_down, norm1_mix_g, conv_w_in, conv_dw_w, conv_dw_b, conv_ln_g, conv_ln_b, conv_w_out, norm1_ffn_g, moe_router, moe_w_gate, moe_w_up, moe_w_down):
    bp, tp, d = x_prompt.shape
    bs, ts, _ = x_sample.shape
    n_p, n_s = bp * tp, bs * DEC_TOK
    cast = lambda a: a.astype(BF16)
    w_in0_b, w_out0_b = cast(w_in0), cast(w_out0)
    ffn_g_b, ffn_u_b, ffn_d_b = cast(ffn_w_gate), cast(ffn_w_up), cast(ffn_w_down)
    conv_in_b, conv_out_b = cast(conv_w_in), cast(conv_w_out)
    moe_g_b, moe_u_b, moe_d_b = tile_expert_weights(moe_w_gate, moe_w_up, moe_w_down, MOE_TF)
    rk = r_k.reshape(-1)
    rwkv_w = (mu_shift, w_lora_up, w0, a_lora_up, a0, g_lora_up, k_k, k_a, rk)

    xp = x_prompt.reshape(n_p, d)
    xs = _pad_tokens(x_sample, DEC_TOK).reshape(n_s, d)

    p_a, q, k, v_p = norm_proj(xp, norm0_mix_g, w_in0_b, IN0_SPLITS, 512)
    p_a3 = p_a.reshape(bp, tp, A_COLS)
    r_, lw_, kt_, vv_, kap_, b_, g_, bonus_ = rwkv_prep(
        p_a3, jnp.zeros((bp, A_COLS), F32), *rwkv_w, 256)
    y_p, wkv_prompt = rwkv_chunked(r_, lw_, kt_, vv_, kap_, b_,
                                   jnp.zeros((bp, A_HEADS, HEAD_DIM, HEAD_DIM), F32), RWKV_CHUNK, tp, bp,
                                   RWKV_PASSES_PREFILL)
    kn_p, km, q_t, k_b, v_t = moba_prep_t(q, k, v_p, q_norm_g, k_norm_g, bp)
    o_p = moba_prefill(q_t, k_b.reshape(bp, tp, B_WIDTH), v_t, km.reshape(bp, tp // MOBA_BLOCK, B_WIDTH))
    flat = lambda a: a.reshape(-1, a.shape[-1])
    xp = mix_out(xp, flat(y_p), flat(bonus_), flat(g_), flat(o_p), lnx_g, lnx_b, w_out0_b, 512)
    xp = ffn(xp, norm0_ffn_g, ffn_g_b, ffn_u_b, ffn_d_b, 512, D_FF // 2)
    shift_prompt = p_a3[:, -1]

    ps_a, q, k, v_s = norm_proj(xs, norm0_mix_g, w_in0, IN0_SPLITS, n_s)
    ps_a3 = ps_a.reshape(bs, DEC_TOK, A_COLS)
    r_, lw_, kt_, vv_, kap_, b_, g_, bonus_ = rwkv_prep(ps_a3, state_l0_shift, *rwkv_w, DEC_TOK)
    y_s, wkv_sample = rwkv_chunked(r_, lw_, kt_, vv_, kap_, b_, state_l0_wkv, DEC_TOK, ts, 2,
                                   RWKV_PASSES_DECODE)
    qn, kn_s, _ = moba_prep(q, k, q_norm_g, k_norm_g, n_s)
    tok3 = lambda a: a.reshape(bs, DEC_TOK, B_WIDTH)
    pool = cache_l0_k.shape[0]
    o_s = moba_decode(tok3(qn), tok3(kn_s), tok3(v_s), cache_l0_k.reshape(pool, PAGE_SIZE, B_WIDTH),
                      cache_l0_v.reshape(pool, PAGE_SIZE, B_WIDTH), page_table, ts)
    xs = mix_out(xs, flat(y_s), flat(bonus_), flat(g_), flat(o_s), lnx_g, lnx_b, w_out0_b, n_s)
    xs = ffn(xs, norm0_ffn_g, ffn_g_b, ffn_u_b, ffn_d_b, n_s, D_FF // 2)
    shift_sample = ps_a3[:, ts - 1]

    u_p = conv_in(xp, norm1_mix_g, conv_in_b, 512).reshape(bp, tp, d)
    xp = conv_dw(u_p, jnp.zeros((bp, HALO, d), F32), xp.reshape(bp, tp, d), conv_dw_w, conv_dw_b,
                 conv_ln_g, conv_ln_b, conv_out_b, 256).reshape(n_p, d)
    xp = moe_layer(xp, norm1_ffn_g, moe_router, moe_g_b, moe_u_b, moe_d_b, MOE_ROWS)
    conv_prompt = u_p[:, tp - (CONV_K - 1):]

    u_s = conv_in(xs, norm1_mix_g, conv_in_b, n_s).reshape(bs, DEC_TOK, d)
    init = jnp.pad(state_l1_conv, ((0, 0), (HALO - (CONV_K - 1), 0), (0, 0)))
    xs = conv_dw(u_s, init, xs.reshape(bs, DEC_TOK, d), conv_dw_w, conv_dw_b,
                 conv_ln_g, conv_ln_b, conv_out_b, DEC_TOK).reshape(n_s, d)
    xs = moe_layer(xs, norm1_ffn_g, moe_router, moe_g_b, moe_u_b, moe_d_b, MOE_ROWS_DECODE)
    conv_sample = jnp.concatenate([state_l1_conv, u_s[:, :ts]], axis=1)[:, -(CONV_K - 1):]

    heads = lambda a, b_, t_: a.reshape(b_, -1, B_WIDTH)[:, :t_].reshape(b_, t_, B_HEADS, HEAD_DIM)
    return (xp.reshape(bp, tp, d), xs.reshape(bs, DEC_TOK, d)[:, :ts],
            heads(kn_p, bp, tp), heads(v_p, bp, tp), heads(kn_s, bs, ts), heads(v_s, bs, ts),
            wkv_prompt, wkv_sample, shift_prompt, shift_sample, conv_prompt, conv_sample)
```

```python
import functools

import jax
import jax.numpy as jnp
from jax import lax
from jax.experimental import pallas as pl
from jax.experimental.pallas import tpu as pltpu

F32 = jnp.float32
BF16 = jnp.bfloat16

D_MODEL = 1024
HEAD_DIM = 64
A_WIDTH = 512
A_HEADS = 8
B_WIDTH = 512
B_HEADS = 8
LORA_DECAY = 64
LORA_ICLR = 64
LORA_GATE = 128
A_COLS = 3 * A_WIDTH + LORA_DECAY + LORA_ICLR + LORA_GATE
MOBA_BLOCK = 256
MOBA_TOPK = 3
PAGE_SIZE = 128
ATTN_SCALE = HEAD_DIM ** -0.5
D_FF = 2816
N_EXPERTS = 8
D_FF_EXPERT = 3584
CONV_K = 31
RMS_EPS = 1e-6
GN_EPS = 64e-5
LN_EPS = 1e-5

LANES = 128
VMEM_LIMIT = 56 * 1024 * 1024
RWKV_CHUNK = 64
MOE_TF = 512
MOE_ROWS = 504
MOE_ROWS_DECODE = 112
HALO = 32
HI = lax.Precision.HIGHEST


def _cparams(sem):
    return pltpu.CompilerParams(dimension_semantics=sem, vmem_limit_bytes=VMEM_LIMIT)


def _rms(x, g):
    return x * lax.rsqrt(jnp.mean(x * x, axis=-1, keepdims=True) + RMS_EPS) * g


def _bdot(a, b):
    return jnp.dot(a.astype(BF16), b.astype(BF16), preferred_element_type=F32)


def _dot_nt(a, b, precision=None):
    return lax.dot_general(a, b, (((1,), (1,)), ((), ())), precision=precision,
                           preferred_element_type=F32)


def _seg_sum(x, bd):
    hi = x.astype(BF16)
    lo = (x - hi.astype(F32)).astype(BF16)
    return (jnp.dot(hi, bd, preferred_element_type=F32)
            + jnp.dot(lo, bd, preferred_element_type=F32))


def _block_diag_ones(n, seg):
    i = jnp.arange(n)
    return (i[:, None] // seg == i[None, :] // seg).astype(BF16)


def _norm_proj_kernel(x_ref, g_ref, w_ref, *out_refs, splits):
    h = _rms(x_ref[...], g_ref[...])
    if w_ref.dtype == F32:
        dot = functools.partial(jnp.dot, h, precision=HI, preferred_element_type=F32)
    else:
        dot = functools.partial(jnp.dot, h.astype(BF16), preferred_element_type=F32)
    for o_ref, (c0, cn) in zip(out_refs, splits):
        o_ref[...] = dot(w_ref[:, c0:c0 + cn])


def norm_proj(x, g, w_bf16, splits, tm):
    n, d = x.shape
    cols = w_bf16.shape[1]
    return pl.pallas_call(
        functools.partial(_norm_proj_kernel, splits=splits),
        grid=(n // tm,),
        in_specs=[pl.BlockSpec((tm, d), lambda i: (i, 0)),
                  pl.BlockSpec((1, d), lambda i: (0, 0)),
                  pl.BlockSpec((d, cols), lambda i: (0, 0))],
        out_specs=[pl.BlockSpec((tm, cn), lambda i: (i, 0)) for _, cn in splits],
        out_shape=[jax.ShapeDtypeStruct((n, cn), F32) for _, cn in splits],
        compiler_params=_cparams(("parallel",)),
        name="norm_proj",
    )(x, g.reshape(1, d), w_bf16)


def _rwkv_prep_kernel(p_ref, shift_ref, mu_ref, wl_ref, w0_ref, al_ref, a0_ref, gl_ref,
                      kk_ref, ka_ref, rk_ref, bd_ref,
                      r_o, lw_o, k_o, v_o, kap_o, b_o, g_o, bonus_o, carry):
    t = pl.program_id(1)

    @pl.when(t == 0)
    def _():
        carry[...] = shift_ref[0]

    p = p_ref[0]
    tt = p.shape[0]
    rolled = pltpu.roll(p, shift=1, axis=0)
    row = lax.broadcasted_iota(jnp.int32, p.shape, 0)
    prev = jnp.where(row == 0, carry[...], rolled)
    carry[...] = p[tt - 1:tt, :]
    xs = p + (prev - p) * mu_ref[...]
    aw = A_WIDTH
    r = xs[:, 0:aw]
    k = xs[:, aw:2 * aw]
    v = xs[:, 2 * aw:3 * aw]
    c0 = 3 * aw
    xw = xs[:, c0:c0 + LORA_DECAY]
    xa = xs[:, c0 + LORA_DECAY:c0 + LORA_DECAY + LORA_ICLR]
    xg = xs[:, c0 + LORA_DECAY + LORA_ICLR:]
    bd = bd_ref[...]

    z = w0_ref[...] + _bdot(jnp.tanh(xw), wl_ref[...])
    nz = -z
    softplus = jnp.maximum(nz, 0.0) + jnp.log(1.0 + jnp.exp(-jnp.abs(nz)))
    w_log = -softplus - 0.5
    lw_o[0] = -jnp.exp(w_log)
    a = jax.nn.sigmoid(a0_ref[...] + _bdot(xa, al_ref[...]))
    g_o[0] = _bdot(jax.nn.sigmoid(xg), gl_ref[...])
    kk = k * kk_ref[...]
    nrm = jnp.sqrt(_seg_sum(kk * kk, bd))
    kap = kk / jnp.maximum(nrm, 1e-12)
    k_h = k * (1.0 + (a - 1.0) * ka_ref[...])
    r_o[0] = r
    k_o[0] = k_h
    v_o[0] = v
    kap_o[0] = kap
    b_o[0] = kap * a
    bonus_o[0] = _seg_sum(r * k_h * rk_ref[...], bd) * v


def rwkv_prep(p_a, shift_prev, mu_shift, w_lora_up, w0, a_lora_up, a0, g_lora_up, k_k, k_a, r_k, tt):
    b, t, _ = p_a.shape
    row = lambda x: x.reshape(1, -1)
    const = lambda shape: pl.BlockSpec(shape, lambda i, j: (0,) * len(shape))
    tok = lambda w: pl.BlockSpec((1, tt, w), lambda i, j: (i, j, 0))
    outs = [jax.ShapeDtypeStruct((b, t, A_WIDTH), F32)] * 8
    return pl.pallas_call(
        _rwkv_prep_kernel,
        grid=(b, t // tt),
        in_specs=[tok(A_COLS),
                  pl.BlockSpec((1, 1, A_COLS), lambda i, j: (i, 0, 0)),
                  const((1, A_COLS)),
                  const((LORA_DECAY, A_WIDTH)), const((1, A_WIDTH)),
                  const((LORA_ICLR, A_WIDTH)), const((1, A_WIDTH)),
                  const((LORA_GATE, A_WIDTH)),
                  const((1, A_WIDTH)), const((1, A_WIDTH)), const((1, A_WIDTH)),
                  const((A_WIDTH, A_WIDTH))],
        out_specs=[tok(A_WIDTH)] * 8,
        out_shape=outs,
        scratch_shapes=[pltpu.VMEM((1, A_COLS), F32)],
        compiler_params=_cparams(("parallel", "arbitrary")),
        name="rwkv_prep",
    )(p_a, shift_prev.reshape(b, 1, A_COLS), row(mu_shift), w_lora_up.astype(BF16), row(w0),
      a_lora_up.astype(BF16), row(a0), g_lora_up.astype(BF16), row(k_k), row(k_a), row(r_k),
      _block_diag_ones(A_WIDTH, HEAD_DIM))


RWKV_PASSES_PREFILL = {"gram": 3, "inv": 1, "apply": 1, "state": 1}
RWKV_PASSES_DECODE = {"gram": 6, "inv": 6, "apply": 6, "state": 6}


def _split(x):
    hi = x.astype(BF16)
    return hi, (x - hi.astype(F32)).astype(BF16)


def _mm(a, b, passes, dims=(1, 0)):
    dn = (((dims[0],), (dims[1],)), ((), ()))
    dot = functools.partial(lax.dot_general, dimension_numbers=dn, preferred_element_type=F32)
    if passes == 6:
        return dot(a, b, precision=HI)
    if passes == 1:
        return dot(a.astype(BF16), b.astype(BF16))
    ah, al = _split(a)
    bh, bl = _split(b)
    return dot(ah, bh) + (dot(ah, bl) + dot(al, bh))


def _unit_lower_inverses(ls, n, c, passes):
    ri = lax.broadcasted_iota(jnp.int32, (n, n), 0)
    ci = lax.broadcasted_iota(jnp.int32, (n, n), 1)
    eye = (ri == ci).astype(F32)

    def same_block(bs):
        return (ri // bs) == (ci // bs)

    b0 = min(16, c)
    powers = [jnp.where(same_block(b0), -l, 0.0) for l in ls]
    invs = [eye + p for p in powers]
    span = 2
    while span < b0:
        powers = [_mm(p, p, passes) for p in powers]
        invs = [i + _mm(i, p, passes) for i, p in zip(invs, powers)]
        span *= 2
    bs = b0
    while bs < c:
        cross = same_block(2 * bs) & jnp.logical_not(same_block(bs))
        mids = [_mm(i, jnp.where(cross, l, 0.0), passes) for i, l in zip(invs, ls)]
        invs = [i - _mm(m, i, passes) for i, m in zip(invs, mids)]
        bs *= 2
    return invs


def _rwkv_chunk_kernel(r_ref, lw_ref, k_ref, v_ref, kap_ref, b_ref, s0_ref, y_ref, sout_ref, state,
                       *, chunk, t_valid, pw):
    ci = pl.program_id(1)
    c = chunk
    n = 2 * c
    hd = HEAD_DIM
    pairs = [(bi, j) for bi in range(r_ref.shape[0]) for j in range(A_HEADS // 2)]

    @pl.when(ci == 0)
    def _():
        z = jnp.zeros((hd, hd), F32)
        for idx, (bi, j) in enumerate(pairs):
            state[idx] = jnp.concatenate(
                [jnp.concatenate([s0_ref[bi, 2 * j], z], axis=1),
                 jnp.concatenate([z, s0_ref[bi, 2 * j + 1]], axis=1)], axis=0)

    lane = lax.broadcasted_iota(jnp.int32, (c, LANES), 1)
    trow = lax.broadcasted_iota(jnp.int32, (c, LANES), 0) + ci * c
    live = trow < t_valid
    m0 = lane < hd
    tri = (lax.broadcasted_iota(jnp.int32, (c, c), 0)
           >= lax.broadcasted_iota(jnp.int32, (c, c), 1)).astype(BF16)
    ri = lax.broadcasted_iota(jnp.int32, (n, n), 0)
    cj = lax.broadcasted_iota(jnp.int32, (n, n), 1)
    same = (ri // c) == (cj // c)
    strict = same & (ri > cj)
    incl = same & (ri >= cj)

    def stack(x):
        return jnp.concatenate([jnp.where(m0, x, 0.0), jnp.where(m0, 0.0, x)], axis=0)

    def cumulative(x):
        hi, lo = _split(x)
        lo2 = (x - hi.astype(F32) - lo.astype(F32)).astype(BF16)
        dot = functools.partial(jnp.dot, preferred_element_type=F32)
        return dot(tri, hi) + (dot(tri, lo) + dot(tri, lo2))

    each = lambda f, *cols: [f(*xs) for xs in zip(*cols)]
    sls = [(bi, slice(j * LANES, (j + 1) * LANES)) for bi, j in pairs]
    ld = lambda ref: [jnp.where(live, ref[bi, :, sl], 0.0) for bi, sl in sls]
    r = [r_ref[bi, :, sl] for bi, sl in sls]
    lw, k, v, kap, b = ld(lw_ref), ld(k_ref), ld(v_ref), ld(kap_ref), ld(b_ref)
    cum = each(cumulative, lw)
    total = [x[c - 1:c, :] for x in cum]
    ginv = [jnp.exp(-x) for x in cum]
    tail = each(lambda t, x: jnp.exp(t - x), total, cum)
    kap_s = each(lambda a, x, w: stack(a * jnp.exp(x - w)), kap, cum, lw)
    r_s = each(lambda a, x: stack(a * jnp.exp(x)), r, cum)
    b_s = each(lambda a, g: stack(a * g), b, ginv)
    k_s = each(lambda a, g: stack(a * g), k, ginv)
    v_s = each(stack, v)
    kbar_s = each(lambda a, t: stack(a * t), k, tail)
    bbar_s = each(lambda a, t: stack(a * t), b, tail)
    gram = lambda x, y: _mm(x, y, pw["gram"], (1, 1))
    l_b = each(lambda x, y: jnp.where(strict, gram(x, y), 0.0), kap_s, b_s)
    l_k = each(lambda x, y: jnp.where(strict, gram(x, y), 0.0), kap_s, k_s)
    a_qk = each(lambda x, y: jnp.where(incl, gram(x, y), 0.0), r_s, k_s)
    a_qb = each(lambda x, y: jnp.where(incl, gram(x, y), 0.0), r_s, b_s)
    inv = _unit_lower_inverses(l_b, n, c, pw["inv"])
    app = lambda x, y: _mm(x, y, pw["apply"])
    w_s = each(app, inv, kap_s)
    u0_s = each(app, inv, each(app, l_k, v_s))
    y0_s = each(app, a_qk, v_s)

    st = lambda x, y, dims=(1, 0): _mm(x, y, pw["state"], dims)
    s = [state[idx] for idx in range(len(pairs))]
    u_s = each(lambda w, s_, u0: st(w, s_, (1, 1)) + u0, w_s, s, u0_s)
    y_s = each(lambda r_, s_, y0, a, u: st(r_, s_, (1, 1)) + y0 - st(a, u), r_s, s, y0_s, a_qb, u_s)
    s_new = each(lambda s_, t, v_, kb, u, bb: s_ * jnp.exp(t) + st(v_, kb, (0, 0)) - st(u, bb, (0, 0)),
                 s, total, v_s, kbar_s, u_s, bbar_s)
    for idx, (bi, sl) in enumerate(sls):
        y_ref[bi, :, sl] = y_s[idx][0:c] + y_s[idx][c:n]
        state[idx] = s_new[idx]

    @pl.when(ci == pl.num_programs(1) - 1)
    def _():
        for idx, (bi, j) in enumerate(pairs):
            s = state[idx]
            sout_ref[bi, 2 * j] = s[0:hd, 0:hd]
            sout_ref[bi, 2 * j + 1] = s[hd:2 * hd, hd:2 * hd]


def rwkv_chunked(r, lw, k, v, kap, b, s0, chunk, t_valid, bb, passes):
    bsz, t, w = r.shape
    tok = pl.BlockSpec((bb, chunk, w), lambda i, c: (i, c, 0))
    st = pl.BlockSpec((bb, A_HEADS, HEAD_DIM, HEAD_DIM), lambda i, c: (i, 0, 0, 0))
    y, s_out = pl.pallas_call(
        functools.partial(_rwkv_chunk_kernel, chunk=chunk, t_valid=t_valid, pw=passes),
        grid=(bsz // bb, t // chunk),
        in_specs=[tok] * 6 + [st],
        out_specs=[tok, st],
        out_shape=[jax.ShapeDtypeStruct((bsz, t, w), F32),
                   jax.ShapeDtypeStruct((bsz, A_HEADS, HEAD_DIM, HEAD_DIM), F32)],
        scratch_shapes=[pltpu.VMEM((bb * A_HEADS // 2, LANES, LANES), F32)],
        compiler_params=_cparams(("parallel", "arbitrary")),
        name="rwkv_chunk",
    )(r, lw, k, v, kap, b, s0)
    return y, s_out


def _mix_out_kernel(x_ref, y_ref, bonus_ref, g_ref, o_ref, lng_ref, lnb_ref, bd_ref, w_ref, out_ref):
    y = y_ref[...]
    bd = bd_ref[...]
    mu = _seg_sum(y, bd) * (1.0 / HEAD_DIM)
    d = y - mu
    var = _seg_sum(d * d, bd) * (1.0 / HEAD_DIM)
    yn = d * lax.rsqrt(var + GN_EPS) * lng_ref[...] + lnb_ref[...]
    ya = (yn + bonus_ref[...]) * g_ref[...]
    acc = jnp.dot(ya.astype(BF16), w_ref[0:A_WIDTH, :], preferred_element_type=F32)
    acc += jnp.dot(o_ref[...].astype(BF16), w_ref[A_WIDTH:, :], preferred_element_type=F32)
    out_ref[...] = x_ref[...] + acc


def mix_out(x, y, bonus, g, o, lnx_g, lnx_b, w_out_bf16, tm):
    n, d = x.shape
    tok = lambda w: pl.BlockSpec((tm, w), lambda i: (i, 0))
    const = lambda shape: pl.BlockSpec(shape, lambda i: (0,) * len(shape))
    return pl.pallas_call(
        _mix_out_kernel,
        grid=(n // tm,),
        in_specs=[tok(d), tok(A_WIDTH), tok(A_WIDTH), tok(A_WIDTH), tok(B_WIDTH),
                  const((1, A_WIDTH)), const((1, A_WIDTH)), const((A_WIDTH, A_WIDTH)),
                  const((A_WIDTH + B_WIDTH, d))],
        out_specs=tok(d),
        out_shape=jax.ShapeDtypeStruct((n, d), F32),
        compiler_params=_cparams(("parallel",)),
        name="mix_out",
    )(x, y, bonus, g, o, lnx_g.reshape(1, -1), lnx_b.reshape(1, -1),
      _block_diag_ones(A_WIDTH, HEAD_DIM), w_out_bf16)


def _ffn_kernel(x_ref, g_ref, wg_ref, wu_ref, wd_ref, out_ref, h_scr, acc):
    j = pl.program_id(1)

    @pl.when(j == 0)
    def _():
        h_scr[...] = _rms(x_ref[...], g_ref[...]).astype(BF16)
        acc[...] = jnp.zeros_like(acc)

    h = h_scr[...]
    gate = jnp.dot(h, wg_ref[...], preferred_element_type=F32)
    up = jnp.dot(h, wu_ref[...], preferred_element_type=F32)
    act = (jax.nn.silu(gate) * up).astype(BF16)
    acc[...] += jnp.dot(act, wd_ref[...], preferred_element_type=F32)

    @pl.when(j == pl.num_programs(1) - 1)
    def _():
        out_ref[...] = x_ref[...] + acc[...]


def ffn(x, g, wg, wu, wd, tm, tf):
    n, d = x.shape
    f = wg.shape[1]
    return pl.pallas_call(
        _ffn_kernel,
        grid=(n // tm, f // tf),
        in_specs=[pl.BlockSpec((tm, d), lambda i, j: (i, 0)),
                  pl.BlockSpec((1, d), lambda i, j: (0, 0)),
                  pl.BlockSpec((d, tf), lambda i, j: (0, j)),
                  pl.BlockSpec((d, tf), lambda i, j: (0, j)),
                  pl.BlockSpec((tf, d), lambda i, j: (j, 0))],
        out_specs=pl.BlockSpec((tm, d), lambda i, j: (i, 0)),
        out_shape=jax.ShapeDtypeStruct((n, d), F32),
        scratch_shapes=[pltpu.VMEM((tm, d), BF16), pltpu.VMEM((tm, d), F32)],
        compiler_params=_cparams(("parallel", "arbitrary")),
        name="ffn",
    )(x, g.reshape(1, d), wg, wu, wd)


def _conv_in_kernel(x_ref, g_ref, w_ref, u_ref):
    h = _rms(x_ref[...], g_ref[...]).astype(BF16)
    d = u_ref.shape[1]
    a = jnp.dot(h, w_ref[:, 0:d], preferred_element_type=F32)
    gate = jnp.dot(h, w_ref[:, d:2 * d], preferred_element_type=F32)
    u_ref[...] = a * jax.nn.sigmoid(gate)


def conv_in(x, g, w_bf16, tm):
    n, d = x.shape
    return pl.pallas_call(
        _conv_in_kernel,
        grid=(n // tm,),
        in_specs=[pl.BlockSpec((tm, d), lambda i: (i, 0)),
                  pl.BlockSpec((1, d), lambda i: (0, 0)),
                  pl.BlockSpec((d, 2 * d), lambda i: (0, 0))],
        out_specs=pl.BlockSpec((tm, d), lambda i: (i, 0)),
        out_shape=jax.ShapeDtypeStruct((n, d), F32),
        compiler_params=_cparams(("parallel",)),
        name="conv_in",
    )(x, g.reshape(1, d), w_bf16)


def _conv_dw_kernel(u_ref, init_ref, x_ref, dw_ref, db_ref, lg_ref, lb_ref, w_ref, out_ref, ctx):
    t = pl.program_id(1)
    tt = u_ref.shape[1]

    @pl.when(t == 0)
    def _():
        ctx[0:HALO, :] = init_ref[0]

    ctx[HALO:HALO + tt, :] = u_ref[0]
    acc = jnp.zeros((tt, D_MODEL), F32) + db_ref[...]
    for j in range(CONV_K):
        acc = acc + ctx[pl.ds(HALO - (CONV_K - 1) + j, tt), :] * dw_ref[j:j + 1, :]
    keep = ctx[tt:tt + HALO, :]
    ctx[0:HALO, :] = keep
    mu = jnp.mean(acc, axis=-1, keepdims=True)
    d = acc - mu
    var = jnp.mean(d * d, axis=-1, keepdims=True)
    cf = d * lax.rsqrt(var + LN_EPS) * lg_ref[...] + lb_ref[...]
    out_ref[0] = x_ref[0] + jnp.dot(jax.nn.silu(cf).astype(BF16), w_ref[...], preferred_element_type=F32)


def conv_dw(u, init, x, dw_w, dw_b, ln_g, ln_b, w_out_bf16, tt):
    b, t, d = u.shape
    dw = jnp.pad(dw_w, ((0, HALO - CONV_K), (0, 0)))
    tok = pl.BlockSpec((1, tt, d), lambda i, j: (i, j, 0))
    const = lambda shape: pl.BlockSpec(shape, lambda i, j: (0,) * len(shape))
    return pl.pallas_call(
        _conv_dw_kernel,
        grid=(b, t // tt),
        in_specs=[tok, pl.BlockSpec((1, HALO, d), lambda i, j: (i, 0, 0)), tok,
                  const((HALO, d)), const((1, d)), const((1, d)), const((1, d)), const((d, d))],
        out_specs=tok,
        out_shape=jax.ShapeDtypeStruct((b, t, d), F32),
        scratch_shapes=[pltpu.VMEM((HALO + tt, d), F32)],
        compiler_params=_cparams(("parallel", "arbitrary")),
        name="conv_dw",
    )(u, init, x, dw, dw_b.reshape(1, d), ln_g.reshape(1, d), ln_b.reshape(1, d), w_out_bf16)


def _router_kernel(x_ref, g_ref, wr_ref, idx_ref, pw_ref):
    h = _rms(x_ref[...], g_ref[...])
    logits = _bdot(h, wr_ref[...])
    lane = lax.broadcasted_iota(jnp.int32, logits.shape, 1)
    valid = lane < N_EXPERTS
    logits = jnp.where(valid, logits, -jnp.inf)
    e = jnp.exp(logits - jnp.max(logits, axis=-1, keepdims=True))
    probs = jnp.where(valid, e / jnp.sum(e, axis=-1, keepdims=True), -1.0)
    m1 = jnp.max(probs, axis=-1, keepdims=True)
    i1 = jnp.min(jnp.where(probs == m1, lane, LANES), axis=-1, keepdims=True)
    rest = jnp.where(lane == i1, -1.0, probs)
    m2 = jnp.max(rest, axis=-1, keepdims=True)
    i2 = jnp.min(jnp.where(rest == m2, lane, LANES), axis=-1, keepdims=True)
    den = m1 + m2
    idx_ref[...] = jnp.where(lane == 0, i1, jnp.where(lane == 1, i2, 0))
    pw_ref[...] = jnp.where(lane == 0, m1 / den, jnp.where(lane == 1, m2 / den, 0.0))


def router(x, g, w_router, tm):
    n, d = x.shape
    wr = jnp.pad(w_router, ((0, 0), (0, LANES - N_EXPERTS))).astype(BF16)
    return pl.pallas_call(
        _router_kernel,
        grid=(n // tm,),
        in_specs=[pl.BlockSpec((tm, d), lambda i: (i, 0)),
                  pl.BlockSpec((1, d), lambda i: (0, 0)),
                  pl.BlockSpec((d, LANES), lambda i: (0, 0))],
        out_specs=[pl.BlockSpec((tm, LANES), lambda i: (i, 0))] * 2,
        out_shape=[jax.ShapeDtypeStruct((n, LANES), jnp.int32), jax.ShapeDtypeStruct((n, LANES), F32)],
        compiler_params=_cparams(("parallel",)),
        name="moe_router",
    )(x, g.reshape(1, d), wr)


def _moe_kernel(be_ref, nu_ref, tok_ref, dst_ref,
                x_hbm, g_ref, p_ref, wg_ref, wu_ref, wd_ref,
                y_hbm, xg, yst, h_scr, acc, sem_in, sem_out, *, nj):
    i = pl.program_id(0)
    j = pl.program_id(1)
    m = xg.shape[1]
    part = m // nj
    n_used = nu_ref[0]
    valid = i < n_used
    cur = i % 2
    oth = (i + 1) % 2
    last_j = j == nj - 1

    def row_in(slot, r, tok):
        return pltpu.make_async_copy(x_hbm.at[pl.ds(tok, 1), :], xg.at[slot, pl.ds(r, 1), :], sem_in.at[slot])

    def row_out(slot, r, dst):
        return pltpu.make_async_copy(yst.at[slot, pl.ds(r, 1), :], y_hbm.at[pl.ds(dst, 1), :], sem_out.at[slot])

    def all_in(slot):
        return pltpu.make_async_copy(x_hbm.at[pl.ds(0, m), :], xg.at[slot], sem_in.at[slot])

    def all_out(slot):
        return pltpu.make_async_copy(yst.at[slot], y_hbm.at[pl.ds(0, m), :], sem_out.at[slot])

    def start_scatter_part():
        for t in range(part):
            r = j * part + t
            row_out(oth, r, dst_ref[(i - 1) * m + r]).start()

    @pl.when((i == 0) & (j == 0))
    def _():
        yst[1] = jnp.zeros((m, yst.shape[2]), F32)
        n_real = y_hbm.shape[0] - 2 * m
        spare = [pltpu.make_async_copy(yst.at[1], y_hbm.at[pl.ds(n_real + k * m, m), :], sem_out.at[1])
                 for k in range(2)]
        for cp in spare:
            cp.start()
        for cp in spare:
            cp.wait()

    @pl.when((i == 0) & (j == 0) & valid)
    def _():
        def start(r, c):
            row_in(0, r, tok_ref[r]).start()
            return c
        lax.fori_loop(0, m, start, 0)

    @pl.when((j == 0) & (i <= n_used) & (n_used > 0))
    def _():
        all_in(cur).wait()

    def compute(scatter_previous):
        @pl.when(j == 0)
        def _():
            h_scr[...] = _rms(xg[cur], g_ref[...]).astype(BF16)
            acc[...] = jnp.zeros_like(acc)

        for t in range(part):
            r = j * part + t
            row_in(oth, r, tok_ref[(i + 1) * m + r]).start()
        if scatter_previous:
            start_scatter_part()
        h = h_scr[...]
        gate = jnp.dot(h, wg_ref[0], preferred_element_type=F32)
        up = jnp.dot(h, wu_ref[0], preferred_element_type=F32)
        act = (jax.nn.silu(gate) * up).astype(BF16)
        acc[...] += jnp.dot(act, wd_ref[0], preferred_element_type=F32)

        @pl.when(last_j)
        def _():
            @pl.when(i >= 2)
            def _():
                all_out(cur).wait()
            yst[cur] = acc[...] * p_ref[...]

    @pl.when(valid & (i == 0))
    def _():
        compute(False)

    @pl.when(valid & (i > 0))
    def _():
        compute(True)

    @pl.when((i == n_used) & (i > 0))
    def _():
        start_scatter_part()

    @pl.when(last_j & (i >= 2) & (i >= n_used) & (i - 2 < n_used))
    def _():
        all_out(cur).wait()


def moe_experts(x, g, block_e, n_used, buf_tok, buf_dst, buf_p, wg, wu, wd, m, tf):
    n, d = x.shape
    n_blocks = block_e.shape[0]
    nj = D_FF_EXPERT // tf
    assert m % nj == 0 and m % 8 == 0

    def jj(i, j, nu):
        return jnp.where(i < nu[0], j, nj - 1)

    grid_spec = pltpu.PrefetchScalarGridSpec(
        num_scalar_prefetch=4,
        grid=(n_blocks, nj),
        in_specs=[pl.BlockSpec(memory_space=pl.ANY),
                  pl.BlockSpec((1, d), lambda i, j, be, nu, tk, ds: (0, 0)),
                  pl.BlockSpec((m, 1), lambda i, j, be, nu, tk, ds: (i, 0)),
                  pl.BlockSpec((1, d, tf), lambda i, j, be, nu, tk, ds: (be[i], 0, jj(i, j, nu))),
                  pl.BlockSpec((1, d, tf), lambda i, j, be, nu, tk, ds: (be[i], 0, jj(i, j, nu))),
                  pl.BlockSpec((1, tf, d), lambda i, j, be, nu, tk, ds: (be[i], jj(i, j, nu), 0))],
        out_specs=pl.BlockSpec(memory_space=pl.ANY),
        scratch_shapes=[pltpu.VMEM((2, m, d), F32), pltpu.VMEM((2, m, d), F32), pltpu.VMEM((m, d), BF16),
                        pltpu.VMEM((m, d), F32), pltpu.SemaphoreType.DMA((2,)), pltpu.SemaphoreType.DMA((2,))],
    )
    return pl.pallas_call(
        functools.partial(_moe_kernel, nj=nj),
        grid_spec=grid_spec,
        out_shape=jax.ShapeDtypeStruct((2 * n + 2 * m, d), F32),
        compiler_params=_cparams(("arbitrary", "arbitrary")),
        name="moe_experts",
    )(block_e, n_used, buf_tok, buf_dst, x, g.reshape(1, d), buf_p, wg, wu, wd)


def _combine_kernel(x_ref, a_ref, b_ref, o_ref):
    o_ref[...] = x_ref[...] + (a_ref[...] + b_ref[...])


def moe_combine(x, y2, tm):
    n, d = x.shape
    half = n // tm
    return pl.pallas_call(
        _combine_kernel,
        grid=(half,),
        in_specs=[pl.BlockSpec((tm, d), lambda i: (i, 0)),
                  pl.BlockSpec((tm, d), lambda i: (i, 0)),
                  pl.BlockSpec((tm, d), lambda i: (i + half, 0))],
        out_specs=pl.BlockSpec((tm, d), lambda i: (i, 0)),
        out_shape=jax.ShapeDtypeStruct((n, d), F32),
        compiler_params=_cparams(("parallel",)),
        name="moe_combine",
    )(x, y2, y2)


def moe_layer(x, g, w_router, wg, wu, wd, m):
    n, d = x.shape
    idx, pw = router(x, g, w_router, min(n, 512))
    flat_e = idx[:, :2].reshape(-1)
    flat_p = pw[:, :2].reshape(-1)
    onehot = (flat_e[:, None] == jnp.arange(N_EXPERTS, dtype=jnp.int32)[None, :]).astype(jnp.int32)
    rank = jnp.sum((jnp.cumsum(onehot, axis=0) - onehot) * onehot, axis=1)
    counts = jnp.sum(onehot, axis=0)
    padded = (counts + m - 1) // m * m
    pend = jnp.cumsum(padded)
    slot = (pend - padded)[flat_e] + rank
    n_blocks = (2 * n) // m + N_EXPERTS + 2
    n_slots = n_blocks * m
    owner = jnp.zeros((n_slots,), jnp.int32).at[slot].set(jnp.arange(1, 2 * n + 1, dtype=jnp.int32))
    pair = jnp.maximum(owner - 1, 0)
    buf_tok = pair // 2
    s_idx = jnp.arange(n_slots, dtype=jnp.int32)
    spare = 2 * n + (s_idx // m % 2) * m + s_idx % m
    buf_dst = jnp.where(owner > 0, (pair % 2) * n + pair // 2, spare)
    buf_p = jnp.where(owner > 0, flat_p[pair], 0.0).reshape(n_slots, 1)
    block_e = jnp.minimum(jnp.searchsorted(pend, jnp.arange(n_blocks, dtype=jnp.int32) * m, side='right'),
                          N_EXPERTS - 1).astype(jnp.int32)
    n_used = (pend[-1] // m).astype(jnp.int32).reshape(1)
    y2 = moe_experts(x, g, block_e, n_used, buf_tok, buf_dst, buf_p, wg, wu, wd, m, MOE_TF)
    return moe_combine(x, y2, min(n, 512))


def _moba_prep_kernel(q_ref, k_ref, qg_ref, kg_ref, bd_ref, qn_ref, kn_ref, km_ref):
    bd = bd_ref[...]

    def head_norm(x, g):
        ms = _seg_sum(x * x, bd) * (1.0 / HEAD_DIM)
        return x * lax.rsqrt(ms + RMS_EPS) * g

    qn_ref[...] = head_norm(q_ref[...], qg_ref[...])
    kn = head_norm(k_ref[...], kg_ref[...])
    kn_ref[...] = kn
    km_ref[0] = jnp.mean(kn, axis=0, keepdims=True)


def moba_prep(q, k, q_norm_g, k_norm_g, tm):
    n, w = q.shape
    tok = pl.BlockSpec((tm, w), lambda i: (i, 0))
    const = lambda shape: pl.BlockSpec(shape, lambda i: (0,) * len(shape))
    return pl.pallas_call(
        _moba_prep_kernel,
        grid=(n // tm,),
        in_specs=[tok, tok, const((1, w)), const((1, w)), const((w, w))],
        out_specs=[tok, tok, pl.BlockSpec((1, 1, w), lambda i: (i, 0, 0))],
        out_shape=[jax.ShapeDtypeStruct((n, w), F32), jax.ShapeDtypeStruct((n, w), F32),
                   jax.ShapeDtypeStruct((n // tm, 1, w), F32)],
        compiler_params=_cparams(("parallel",)),
        name="moba_prep",
    )(q, k, jnp.tile(q_norm_g, B_HEADS).reshape(1, w), jnp.tile(k_norm_g, B_HEADS).reshape(1, w),
      _block_diag_ones(w, HEAD_DIM))


def _top_blocks(gate, bidx, nb):
    chosen = jnp.zeros(gate.shape, F32)
    for _ in range(MOBA_TOPK):
        mx = jnp.max(gate, axis=-1, keepdims=True)
        first = jnp.min(jnp.where(gate == mx, bidx, nb), axis=-1, keepdims=True)
        chosen = jnp.where((bidx == first) & (mx > -jnp.inf), 1.0, chosen)
        gate = jnp.where(bidx == first, -jnp.inf, gate)
    return chosen


def _moba_prep_t_kernel(q_ref, k_ref, v_ref, qg_ref, kg_ref, bd_ref, kn_ref, km_ref, qt_ref, kb_ref, vt_ref):
    bd = bd_ref[...]

    def head_norm(x, g):
        ms = _seg_sum(x * x, bd) * (1.0 / HEAD_DIM)
        return x * lax.rsqrt(ms + RMS_EPS) * g

    qn = head_norm(q_ref[...], qg_ref[...])
    kn = head_norm(k_ref[...], kg_ref[...])
    kn_ref[...] = kn
    kb_ref[...] = kn.astype(BF16)
    km_ref[0] = jnp.mean(kn, axis=0, keepdims=True)
    qt_ref[0] = (qn * ATTN_SCALE).T.astype(BF16)
    vt_ref[0] = v_ref[...].T.astype(BF16)


def moba_prep_t(q, k, v, q_norm_g, k_norm_g, bsz):
    n, w = q.shape
    tm = MOBA_BLOCK
    nblk = n // bsz // tm
    tok = pl.BlockSpec((tm, w), lambda i: (i, 0))
    feat = pl.BlockSpec((1, w, tm), lambda i: (i // nblk, 0, i % nblk))
    const = lambda shape: pl.BlockSpec(shape, lambda i: (0,) * len(shape))
    return pl.pallas_call(
        _moba_prep_t_kernel,
        grid=(n // tm,),
        in_specs=[tok, tok, tok, const((1, w)), const((1, w)), const((w, w))],
        out_specs=[tok, pl.BlockSpec((1, 1, w), lambda i: (i, 0, 0)), feat, tok, feat],
        out_shape=[jax.ShapeDtypeStruct((n, w), F32), jax.ShapeDtypeStruct((n // tm, 1, w), F32),
                   jax.ShapeDtypeStruct((bsz, w, n // bsz), BF16), jax.ShapeDtypeStruct((n, w), BF16),
                   jax.ShapeDtypeStruct((bsz, w, n // bsz), BF16)],
        compiler_params=_cparams(("parallel",)),
        name="moba_prep_t",
    )(q, k, v, jnp.tile(q_norm_g, B_HEADS).reshape(1, w), jnp.tile(k_norm_g, B_HEADS).reshape(1, w),
      _block_diag_ones(w, HEAD_DIM))


def _moba_attn_kernel(qi_ref, ki_ref, qt_ref, k_ref, vt_ref, km_ref, o_ref,
                      qz, sel, bias, s_scr, m_scr, l_scr, acc):
    s = pl.program_id(1)
    qi = qi_ref[s]
    ki = ki_ref[s]
    blk = MOBA_BLOCK
    nb = km_ref.shape[1]
    hd = HEAD_DIM
    is_own = ki == qi
    key_i = lax.broadcasted_iota(jnp.int32, (blk, blk), 0)
    qry_i = lax.broadcasted_iota(jnp.int32, (blk, blk), 1)
    rel = key_i - qry_i
    blk_i = lax.broadcasted_iota(jnp.int32, (nb, blk), 0)
    step_off = jnp.full((1, 1), (qi - ki) * blk, jnp.int32).astype(F32)

    slopes = [2.0 ** (-8.0 * (h + 1) / B_HEADS) for h in range(B_HEADS)]

    def scores(h, own):
        hp = h // 2
        raw = jnp.dot(k_ref[0, :, hp * LANES:(hp + 1) * LANES], qz[h], preferred_element_type=F32)
        sp = raw + bias[h]
        if own:
            sp = jnp.where(rel <= 0, sp, -jnp.inf)
        s_scr[h] = sp
        return jnp.max(sp, axis=0, keepdims=True)

    def attend(h, own, top):
        if own:
            m_new = top
            p = jnp.exp(s_scr[h] - m_new)
            l_scr[h] = jnp.sum(p, axis=0, keepdims=True)
        else:
            live = sel[h, pl.ds(ki, 1), :] > 0.5
            off = slopes[h] * step_off
            m_prev = m_scr[h]
            m_new = jnp.maximum(m_prev, jnp.where(live, top - off, -jnp.inf))
            alpha = jnp.exp(m_prev - m_new)
            p = jnp.exp(s_scr[h] - jnp.where(live, m_new + off, jnp.inf))
            l_scr[h] = alpha * l_scr[h] + jnp.sum(p, axis=0, keepdims=True)
        pv = jnp.dot(vt_ref[0, h * hd:(h + 1) * hd, :], p.astype(BF16), preferred_element_type=F32)
        acc[h] = pv if own else alpha * acc[h] + pv
        m_scr[h] = m_new

    def all_heads(own):
        tops = [scores(h, own) for h in range(B_HEADS)]
        for h in range(B_HEADS):
            attend(h, own, tops[h])

    @pl.when(is_own)
    def _():
        frow = lax.broadcasted_iota(jnp.int32, (LANES, blk), 0)
        for h in range(B_HEADS):
            hp = h // 2
            qt = qt_ref[0, hp * LANES:(hp + 1) * LANES, :]
            mine = (frow < hd) if h % 2 == 0 else (frow >= hd)
            qh = jnp.where(mine, qt, jnp.zeros_like(qt))
            qz[h] = qh
            gate = jnp.dot(km_ref[0, :, hp * LANES:(hp + 1) * LANES].astype(BF16), qh,
                           preferred_element_type=F32)
            gate = jnp.where(blk_i < qi, gate, -jnp.inf)
            chosen = jnp.zeros((nb, blk), F32)
            for _ in range(MOBA_TOPK):
                mx = jnp.max(gate, axis=0, keepdims=True)
                first = jnp.min(jnp.where(gate == mx, blk_i, nb), axis=0, keepdims=True)
                chosen = jnp.where((blk_i == first) & (mx > -jnp.inf), 1.0, chosen)
                gate = jnp.where(blk_i == first, -jnp.inf, gate)
            sel[h] = chosen
            bias[h] = rel.astype(F32) * slopes[h]
        all_heads(True)

    @pl.when(jnp.logical_not(is_own))
    def _():
        all_heads(False)

    @pl.when((ki == qi - 1) | (qi == 0))
    def _():
        for hp in range(B_HEADS // 2):
            pair = jnp.concatenate([acc[2 * hp] / l_scr[2 * hp], acc[2 * hp + 1] / l_scr[2 * hp + 1]], axis=0)
            o_ref[0, :, hp * LANES:(hp + 1) * LANES] = pair.T


def moba_prefill(qt, kb, vt, km):
    b, w, t = qt.shape
    blk = MOBA_BLOCK
    nb = t // blk
    pairs = [(qi, ki) for qi in range(nb) for ki in [qi] + list(range(qi))]
    qi_tab = jnp.array([p[0] for p in pairs], jnp.int32)
    ki_tab = jnp.array([p[1] for p in pairs], jnp.int32)
    grid_spec = pltpu.PrefetchScalarGridSpec(
        num_scalar_prefetch=2,
        grid=(b, len(pairs)),
        in_specs=[pl.BlockSpec((1, w, blk), lambda i, s, qt_, kt_: (i, 0, qt_[s])),
                  pl.BlockSpec((1, blk, w), lambda i, s, qt_, kt_: (i, kt_[s], 0)),
                  pl.BlockSpec((1, w, blk), lambda i, s, qt_, kt_: (i, 0, kt_[s])),
                  pl.BlockSpec((1, nb, w), lambda i, s, qt_, kt_: (i, 0, 0))],
        out_specs=pl.BlockSpec((1, blk, w), lambda i, s, qt_, kt_: (i, qt_[s], 0)),
        scratch_shapes=[pltpu.VMEM((B_HEADS, LANES, blk), BF16), pltpu.VMEM((B_HEADS, nb, blk), F32),
                        pltpu.VMEM((B_HEADS, blk, blk), F32), pltpu.VMEM((B_HEADS, blk, blk), F32),
                        pltpu.VMEM((B_HEADS, 1, blk), F32),
                        pltpu.VMEM((B_HEADS, 1, blk), F32), pltpu.VMEM((B_HEADS, HEAD_DIM, blk), F32)],
    )
    return pl.pallas_call(
        _moba_attn_kernel,
        grid_spec=grid_spec,
        out_shape=jax.ShapeDtypeStruct((b, t, w), F32),
        compiler_params=_cparams(("parallel", "arbitrary")),
        name="moba_prefill",
    )(qi_tab, ki_tab, qt, kb, vt, km)


DEC_TOK = 8
DEC_PAGES = 32


def _moba_dec_kernel(pt_ref, q_ref, kn_ref, vn_ref, *refs, t_new, past_len):
    npg = DEC_PAGES
    k_refs, v_refs = refs[:npg], refs[npg:2 * npg]
    o_ref, kc, km_scr, sel, s_scr, m_scr, l_scr, acc = refs[2 * npg:]
    ph = pl.program_id(1)
    g = pl.program_id(2)
    last_g = g == pl.num_programs(2) - 1
    nh = B_HEADS
    rows = nh * DEC_TOK
    w = B_WIDTH
    per_blk = MOBA_BLOCK // PAGE_SIZE
    nb = km_scr.shape[0]

    qt = jnp.concatenate([q_ref[0]] * nh, axis=0)
    own_lanes = (lax.broadcasted_iota(jnp.int32, (rows, w), 1) // HEAD_DIM
                 == lax.broadcasted_iota(jnp.int32, (rows, w), 0) // DEC_TOK)
    qf = jnp.where(own_lanes, qt, 0.0)
    qb = (qf * ATTN_SCALE).astype(BF16)
    rcol = lax.broadcasted_iota(jnp.int32, (rows, 1), 0)
    tok = rcol % DEC_TOK
    slope = jnp.exp2(-8.0 * (rcol // DEC_TOK + 1).astype(F32) / nh)
    bidx = lax.broadcasted_iota(jnp.int32, (rows, nb), 1)

    @pl.when(ph == 0)
    def _():
        for i in range(0, npg, per_blk):
            tot = jnp.zeros((1, w), F32)
            for u in range(per_blk):
                kp = k_refs[i + u][0]
                kc[g * npg + i + u] = kp.astype(BF16)
                tot = tot + jnp.sum(kp, axis=0, keepdims=True)
            km_scr[pl.ds(g * (npg // per_blk) + i // per_blk, 1), :] = tot * (1.0 / MOBA_BLOCK)

        @pl.when(last_g)
        def _():
            gate = _dot_nt(qf, km_scr[...], HI)
            sel[...] = _top_blocks(gate, bidx, nb)
            c = lax.broadcasted_iota(jnp.int32, (rows, DEC_TOK), 1)
            sc = _dot_nt(qb, kn_ref[0].astype(BF16)) - slope * (tok - c).astype(F32)
            sc = jnp.where((c <= tok) & (c < t_new), sc, -jnp.inf)
            m_new = jnp.max(sc, axis=-1, keepdims=True)
            p = jnp.exp(sc - m_new)
            m_scr[...] = m_new
            l_scr[...] = jnp.sum(p, axis=-1, keepdims=True)
            acc[...] = jnp.dot(p.astype(BF16), vn_ref[0].astype(BF16), preferred_element_type=F32)

    @pl.when(ph == 1)
    def _():
        key_bias = slope * lax.broadcasted_iota(jnp.int32, (rows, PAGE_SIZE), 1).astype(F32)
        tops = []
        for i in range(npg):
            page = g * npg + i
            picked = jnp.sum(jnp.where(bidx == page // per_blk, sel[...], 0.0),
                             axis=-1, keepdims=True) > 0.5
            dist0 = (past_len + tok - page * PAGE_SIZE).astype(F32)
            sc = _dot_nt(qb, kc[page]) + key_bias + jnp.where(picked, -slope * dist0, -jnp.inf)
            s_scr[i] = sc
            tops.append(jnp.max(sc, axis=-1, keepdims=True))
        m_prev = m_scr[...]
        m_new = functools.reduce(jnp.maximum, tops, m_prev)
        alpha = jnp.exp(m_prev - m_new)
        lsum = jnp.zeros((rows, 1), F32)
        pv = jnp.zeros((rows, w), F32)
        for i in range(npg):
            p = jnp.exp(s_scr[i] - m_new)
            lsum = lsum + jnp.sum(p, axis=-1, keepdims=True)
            pv = pv + jnp.dot(p.astype(BF16), v_refs[i][0].astype(BF16), preferred_element_type=F32)
        l_scr[...] = alpha * l_scr[...] + lsum
        acc[...] = alpha * acc[...] + pv
        m_scr[...] = m_new

        @pl.when(last_g)
        def _():
            o = jnp.where(own_lanes, acc[...] / l_scr[...], 0.0)
            o_ref[0] = jnp.sum(o.reshape(nh, DEC_TOK, w), axis=0)


def moba_decode(qn, kn, vn, cache_k, cache_v, page_table, t_new):
    s, _, w = qn.shape
    n_pages = page_table.shape[1]
    past_len = n_pages * PAGE_SIZE
    assert past_len % MOBA_BLOCK == 0 and n_pages % DEC_PAGES == 0 and t_new <= DEC_TOK
    nb = past_len // MOBA_BLOCK
    rows = B_HEADS * DEC_TOK
    new = pl.BlockSpec((1, DEC_TOK, w), lambda b, ph, g, pt: (b, 0, 0))
    last = n_pages // DEC_PAGES - 1

    def k_spec(i):
        return pl.BlockSpec((1, PAGE_SIZE, w),
                            lambda b, ph, g, pt: (pt[b, jnp.where(ph == 0, g, last) * DEC_PAGES + i], 0, 0))

    def v_spec(i):
        return pl.BlockSpec((1, PAGE_SIZE, w),
                            lambda b, ph, g, pt: (pt[b, jnp.where(ph == 0, 0, g) * DEC_PAGES + i], 0, 0))

    grid_spec = pltpu.PrefetchScalarGridSpec(
        num_scalar_prefetch=1,
        grid=(s, 2, n_pages // DEC_PAGES),
        in_specs=[new, new, new] + [k_spec(i) for i in range(DEC_PAGES)] + [v_spec(i) for i in range(DEC_PAGES)],
        out_specs=new,
        scratch_shapes=[pltpu.VMEM((n_pages, PAGE_SIZE, w), BF16), pltpu.VMEM((nb, w), F32),
                        pltpu.VMEM((rows, nb), F32), pltpu.VMEM((DEC_PAGES, rows, PAGE_SIZE), F32),
                        pltpu.VMEM((rows, 1), F32), pltpu.VMEM((rows, 1), F32), pltpu.VMEM((rows, w), F32)],
    )
    return pl.pallas_call(
        functools.partial(_moba_dec_kernel, t_new=t_new, past_len=past_len),
        grid_spec=grid_spec,
        out_shape=jax.ShapeDtypeStruct((s, DEC_TOK, w), F32),
        compiler_params=_cparams(("parallel", "arbitrary", "arbitrary")),
        name="moba_decode",
    )(page_table, qn, kn, vn, *([cache_k] * DEC_PAGES), *([cache_v] * DEC_PAGES))


IN0_SPLITS = ((0, A_COLS), (A_COLS, B_WIDTH), (A_COLS + B_WIDTH, B_WIDTH), (A_COLS + 2 * B_WIDTH, B_WIDTH))


def _pad_tokens(a, t_pad):
    return jnp.pad(a, ((0, 0), (0, t_pad - a.shape[1]), (0, 0)))


def kernel(x_prompt, x_sample, cache_l0_k, cache_l0_v, state_l0_wkv, state_l0_shift, state_l1_conv, page_table, norm0_mix_g, w_in0, mu_shift, w_lora_up, w0, a_lora_up, a0, g_lora_up, k_k, k_a, r_k, lnx_g, lnx_b, q_norm_g, k_norm_g, w_out0, norm0_ffn_g, ffn_w_gate, ffn_w_up, ffn_w_down, norm1_mix_g, conv_w_in, conv_dw_w, conv_dw_b, conv_ln_g, conv_ln_b, conv_w_out, norm1_ffn_g, moe_router, moe_w_gate, moe_w_up, moe_w_down):
    bp, tp, d = x_prompt.shape
    bs, ts, _ = x_sample.shape
    n_p, n_s = bp * tp, bs * DEC_TOK
    cast = lambda a: a.astype(BF16)
    w_in0_b, w_out0_b = cast(w_in0), cast(w_out0)
    ffn_g_b, ffn_u_b, ffn_d_b = cast(ffn_w_gate), cast(ffn_w_up), cast(ffn_w_down)
    conv_in_b, conv_out_b = cast(conv_w_in), cast(conv_w_out)
    moe_g_b, moe_u_b, moe_d_b = cast(moe_w_gate), cast(moe_w_up), cast(moe_w_down)
    rk = r_k.reshape(-1)
    rwkv_w = (mu_shift, w_lora_up, w0, a_lora_up, a0, g_lora_up, k_k, k_a, rk)

    xp = x_prompt.reshape(n_p, d)
    xs = _pad_tokens(x_sample, DEC_TOK).reshape(n_s, d)

    p_a, q, k, v_p = norm_proj(xp, norm0_mix_g, w_in0_b, IN0_SPLITS, 512)
    p_a3 = p_a.reshape(bp, tp, A_COLS)
    r_, lw_, kt_, vv_, kap_, b_, g_, bonus_ = rwkv_prep(
        p_a3, jnp.zeros((bp, A_COLS), F32), *rwkv_w, 256)
    y_p, wkv_prompt = rwkv_chunked(r_, lw_, kt_, vv_, kap_, b_,
                                   jnp.zeros((bp, A_HEADS, HEAD_DIM, HEAD_DIM), F32), RWKV_CHUNK, tp, bp,
                                   RWKV_PASSES_PREFILL)
    kn_p, km, q_t, k_b, v_t = moba_prep_t(q, k, v_p, q_norm_g, k_norm_g, bp)
    o_p = moba_prefill(q_t, k_b.reshape(bp, tp, B_WIDTH), v_t, km.reshape(bp, tp // MOBA_BLOCK, B_WIDTH))
    flat = lambda a: a.reshape(-1, a.shape[-1])
    xp = mix_out(xp, flat(y_p), flat(bonus_), flat(g_), flat(o_p), lnx_g, lnx_b, w_out0_b, 512)
    xp = ffn(xp, norm0_ffn_g, ffn_g_b, ffn_u_b, ffn_d_b, 512, D_FF // 2)
    shift_prompt = p_a3[:, -1]

    ps_a, q, k, v_s = norm_proj(xs, norm0_mix_g, w_in0, IN0_SPLITS, n_s)
    ps_a3 = ps_a.reshape(bs, DEC_TOK, A_COLS)
    r_, lw_, kt_, vv_, kap_, b_, g_, bonus_ = rwkv_prep(ps_a3, state_l0_shift, *rwkv_w, DEC_TOK)
    y_s, wkv_sample = rwkv_chunked(r_, lw_, kt_, vv_, kap_, b_, state_l0_wkv, DEC_TOK, ts, 2,
                                   RWKV_PASSES_DECODE)
    qn, kn_s, _ = moba_prep(q, k, q_norm_g, k_norm_g, n_s)
    tok3 = lambda a: a.reshape(bs, DEC_TOK, B_WIDTH)
    pool = cache_l0_k.shape[0]
    o_s = moba_decode(tok3(qn), tok3(kn_s), tok3(v_s), cache_l0_k.reshape(pool, PAGE_SIZE, B_WIDTH),
                      cache_l0_v.reshape(pool, PAGE_SIZE, B_WIDTH), page_table, ts)
    xs = mix_out(xs, flat(y_s), flat(bonus_), flat(g_), flat(o_s), lnx_g, lnx_b, w_out0_b, n_s)
    xs = ffn(xs, norm0_ffn_g, ffn_g_b, ffn_u_b, ffn_d_b, n_s, D_FF // 2)
    shift_sample = ps_a3[:, ts - 1]

    u_p = conv_in(xp, norm1_mix_g, conv_in_b, 512).reshape(bp, tp, d)
    xp = conv_dw(u_p, jnp.zeros((bp, HALO, d), F32), xp.reshape(bp, tp, d), conv_dw_w, conv_dw_b,
                 conv_ln_g, conv_ln_b, conv_out_b, 256).reshape(n_p, d)
    xp = moe_layer(xp, norm1_ffn_g, moe_router, moe_g_b, moe_u_b, moe_d_b, MOE_ROWS)
    conv_prompt = u_p[:, tp - (CONV_K - 1):]

    u_s = conv_in(xs, norm1_mix_g, conv_in_b, n_s).reshape(bs, DEC_TOK, d)
    init = jnp.pad(state_l1_conv, ((0, 0), (HALO - (CONV_K - 1), 0), (0, 0)))
    xs = conv_dw(u_s, init, xs.reshape(bs, DEC_TOK, d), conv_dw_w, conv_dw_b,
                 conv_ln_g, conv_ln_b, conv_out_b, DEC_TOK).reshape(n_s, d)
    xs = moe_layer(xs, norm1_ffn_g, moe_router, moe_g_b, moe_u_b, moe_d_b, MOE_ROWS_DECODE)
    conv_sample = jnp.concatenate([state_l1_conv, u_s[:, :ts]], axis=1)[:, -(CONV_K - 1):]

    heads = lambda a, b_, t_: a.reshape(b_, -1, B_WIDTH)[:, :t_].reshape(b_, t_, B_HEADS, HEAD_DIM)
    return (xp.reshape(bp, tp, d), xs.reshape(bs, DEC_TOK, d)[:, :ts],
            heads(kn_p, bp, tp), heads(v_p, bp, tp), heads(kn_s, bs, ts), heads(v_s, bs, ts),
            wkv_prompt, wkv_sample, shift_prompt, shift_sample, conv_prompt, conv_sample)
```
